```python
import math
import jax, jax.numpy as jnp
from jax import lax
import numpy as np

D_MODEL = 1024
BATCH = 4
SEQ = 4096
DEPTH = 2

HEAD_DIM = 64
N_ATTN_HEADS = 8
ATTN_WIDTH = N_ATTN_HEADS * HEAD_DIM
LRU_WIDTH = D_MODEL - ATTN_WIDTH
N_LRU_BLOCKS = 8
LRU_BLOCK = LRU_WIDTH // N_LRU_BLOCKS
MIX_WIDTH = ATTN_WIDTH + LRU_WIDTH
IN_WIDTH = 3 * ATTN_WIDTH + 2 * LRU_WIDTH
CONV_WIDTH = 4
LRU_C = 8.0
D_FF = 4 * D_MODEL
BLOCK_Q = 128
EPS = 1e-6

kernel_name = 'hymba_style_rglru_stickbreaking_block'


def rms_norm(x, g):
    xf = x.astype(jnp.float32)
    y = xf * lax.rsqrt(jnp.mean(xf * xf, axis=-1, keepdims=True) + EPS)
    return (y * g.astype(jnp.float32)).astype(x.dtype)


def causal_depthwise_conv(u, w, b):
    S = u.shape[1]
    up = jnp.pad(u, ((0, 0), (CONV_WIDTH - 1, 0), (0, 0)))
    out = b
    for k in range(CONV_WIDTH):
        out = out + up[:, k:k + S] * w[k]
    return out


def _linear_rec_combine(left, right):
    a1, b1 = left
    a2, b2 = right
    return a1 * a2, a2 * b1 + b2


def rg_lru(u, w_rg, b_rg, w_ig, b_ig, lam):
    B, S, _ = u.shape
    ub = u.reshape(B, S, N_LRU_BLOCKS, LRU_BLOCK)
    r = jax.nn.sigmoid(jnp.einsum('bsnc,ncd->bsnd', ub, w_rg) + b_rg.reshape(N_LRU_BLOCKS, LRU_BLOCK))
    i = jax.nn.sigmoid(jnp.einsum('bsnc,ncd->bsnd', ub, w_ig) + b_ig.reshape(N_LRU_BLOCKS, LRU_BLOCK))
    r = r.reshape(B, S, LRU_WIDTH).astype(jnp.float32)
    i = i.reshape(B, S, LRU_WIDTH).astype(jnp.float32)
    log_a = -LRU_C * r * jax.nn.softplus(-lam.astype(jnp.float32))
    a = jnp.exp(log_a)
    mult = jnp.sqrt(-jnp.expm1(2.0 * log_a))
    bterm = mult * (i * u.astype(jnp.float32))
    _, h = lax.associative_scan(_linear_rec_combine, (a, bterm), axis=1)
    return h.astype(u.dtype)


def stick_breaking_attention(q, k, v):
    S = q.shape[1]
    scale = 1.0 / math.sqrt(HEAD_DIM)
    qf = q.astype(jnp.float32)
    kf = k.astype(jnp.float32)
    vf = v.astype(jnp.float32)
    outs = []
    for blk in range(S // BLOCK_Q):
        q0 = blk * BLOCK_Q
        end = q0 + BLOCK_Q
        z = jnp.einsum('bqhd,bkhd->bhqk', qf[:, q0:end], kf[:, :end]) * scale
        qpos = q0 + jnp.arange(BLOCK_Q)[:, None]
        kpos = jnp.arange(end)[None, :]
        mask = kpos < qpos
        log_one_minus = jnp.where(mask, jax.nn.log_sigmoid(-z), 0.0)
        suffix = lax.cumsum(log_one_minus, axis=3, reverse=True) - log_one_minus
        log_w = jax.nn.log_sigmoid(z) + suffix
        w = jnp.where(mask, jnp.exp(log_w), 0.0)
        outs.append(jnp.einsum('bhqk,bkhd->bqhd', w, vf[:, :end]))
    return jnp.concatenate(outs, axis=1).astype(v.dtype)


def hybrid_mixer(h, w_in, conv_w, conv_b, w_rg, b_rg, w_ig, b_ig, lam, q_norm_g, k_norm_g, w_out):
    B, S, _ = h.shape
    proj = h @ w_in
    q, k, v, xl, gl = jnp.split(proj, [ATTN_WIDTH, 2 * ATTN_WIDTH, 3 * ATTN_WIDTH, 3 * ATTN_WIDTH + LRU_WIDTH], axis=-1)
    q = rms_norm(q.reshape(B, S, N_ATTN_HEADS, HEAD_DIM), q_norm_g)
    k = rms_norm(k.reshape(B, S, N_ATTN_HEADS, HEAD_DIM), k_norm_g)
    v = v.reshape(B, S, N_ATTN_HEADS, HEAD_DIM)
    attn = stick_breaking_attention(q, k, v).reshape(B, S, ATTN_WIDTH)
    xl = causal_depthwise_conv(xl, conv_w, conv_b)
    lru = rg_lru(xl, w_rg, b_rg, w_ig, b_ig, lam) * jax.nn.gelu(gl)
    return jnp.concatenate([attn, lru], axis=-1) @ w_out


def sq_relu_mlp(h, w_up, w_down):
    return jnp.square(jax.nn.relu(h @ w_up)) @ w_down


def setup_inputs(seed: int = 0) -> dict:
    key = jax.random.key(seed)
    ks = jax.random.split(key, 17)
    f32 = jnp.float32
    nrm = lambda kk, shape, s: jax.random.normal(kk, shape, f32) * s
    a0 = jax.random.uniform(ks[10], (DEPTH, LRU_WIDTH), f32, minval=0.9, maxval=0.999)
    return {
        'x': jax.random.normal(ks[0], (BATCH, SEQ, D_MODEL), f32),
        'norm1_g': 1.0 + nrm(ks[1], (DEPTH, D_MODEL), 0.02),
        'w_in': nrm(ks[2], (DEPTH, D_MODEL, IN_WIDTH), D_MODEL ** -0.5),
        'conv_w': nrm(ks[3], (DEPTH, CONV_WIDTH, LRU_WIDTH), CONV_WIDTH ** -0.5),
        'conv_b': nrm(ks[4], (DEPTH, LRU_WIDTH), 0.01),
        'w_rg': nrm(ks[5], (DEPTH, N_LRU_BLOCKS, LRU_BLOCK, LRU_BLOCK), LRU_BLOCK ** -0.5),
        'b_rg': nrm(ks[6], (DEPTH, LRU_WIDTH), 0.01),
        'w_ig': nrm(ks[7], (DEPTH, N_LRU_BLOCKS, LRU_BLOCK, LRU_BLOCK), LRU_BLOCK ** -0.5),
        'b_ig': nrm(ks[8], (DEPTH, LRU_WIDTH), 0.01),
        'lru_lambda': jnp.log(a0) - jnp.log1p(-a0),
        'q_norm_g': 1.0 + nrm(ks[11], (DEPTH, HEAD_DIM), 0.02),
        'k_norm_g': 1.0 + nrm(ks[12], (DEPTH, HEAD_DIM), 0.02),
        'w_out': nrm(ks[13], (DEPTH, MIX_WIDTH, D_MODEL), MIX_WIDTH ** -0.5),
        'norm2_g': 1.0 + nrm(ks[14], (DEPTH, D_MODEL), 0.02),
        'w_up': nrm(ks[15], (DEPTH, D_MODEL, D_FF), D_MODEL ** -0.5),
        'w_down': nrm(ks[16], (DEPTH, D_FF, D_MODEL), D_FF ** -0.5),
    }


def reference(x, norm1_g, w_in, conv_w, conv_b, w_rg, b_rg, w_ig, b_ig, lru_lambda, q_norm_g, k_norm_g, w_out, norm2_g, w_up, w_down):
    for l in range(DEPTH):
        h = rms_norm(x, norm1_g[l])
        x = x + hybrid_mixer(h, w_in[l], conv_w[l], conv_b[l], w_rg[l], b_rg[l], w_ig[l], b_ig[l],
                             lru_lambda[l], q_norm_g[l], k_norm_g[l], w_out[l])
        h = rms_norm(x, norm2_g[l])
        x = x + sq_relu_mlp(h, w_up[l], w_down[l])
    return x
```

```python
import functools
import math

import jax
import jax.numpy as jnp
from jax import lax
from jax.experimental import pallas as pl
from jax.experimental.pallas import tpu as pltpu

D_MODEL = 1024
HEAD_DIM = 64
ATTN_WIDTH = 512
LRU_WIDTH = 512
N_LRU_BLOCKS = 8
LRU_BLOCK = 64
IN_WIDTH = 3 * ATTN_WIDTH + 2 * LRU_WIDTH
CONV_WIDTH = 4
LRU_C = 8.0
D_FF = 4 * D_MODEL
EPS = 1e-6

LANES = 128
SUBLANES = 8
VMEM_LIMIT = 56 * 1024 * 1024

TM_IN = 512
SEG = TM_IN // SUBLANES
TM_MLP = 1024
FF_CHUNK = 1024
TQ = 256
TK = 256
HALF = TK // 2
ROW_CHUNK = 64

F32 = jnp.float32
BF16 = jnp.bfloat16


def _dot(a, b):
    return jnp.dot(a, b, preferred_element_type=F32)


def _dot_nt(a, b):
    return lax.dot_general(a, b, (((1,), (1,)), ((), ())), preferred_element_type=F32)


def _softplus(x):
    return jnp.maximum(x, 0.0) + jnp.log(1.0 + jnp.exp(-jnp.abs(x)))


def _mixer_in_kernel(x_ref, g1_ref, win_ref, qg_ref, kg_ref, pavg_ref, cw_ref, cb_ref,
                     wrg_ref, brg_ref, wig_ref, big_ref, lam_ref,
                     q_ref, k_ref, v_ref, lru_ref,
                     xbuf, a_s, b_s, p_s, h_s, state_s):
    t = pl.program_id(1)

    @pl.when(t == 0)
    def _():
        xbuf[0:SUBLANES, :] = jnp.zeros((SUBLANES, LRU_WIDTH), F32)
        state_s[...] = jnp.zeros_like(state_s)

    x = x_ref[0]
    ms = jnp.mean(x * x, axis=-1, keepdims=True)
    y = ((x * lax.rsqrt(ms + EPS)) * g1_ref[...]).astype(BF16)

    def proj(c):
        return _dot(y, win_ref[:, c * ATTN_WIDTH:(c + 1) * ATTN_WIDTH])

    pavg = pavg_ref[...]
    for c, (gain_ref, o_ref) in enumerate(((qg_ref, q_ref), (kg_ref, k_ref))):
        pr = proj(c)
        for j in range(ATTN_WIDTH // LANES):
            ps = pr[:, j * LANES:(j + 1) * LANES]
            sq = ps * ps
            sq_hi = sq.astype(BF16)
            sq_lo = (sq - sq_hi.astype(F32)).astype(BF16)
            m = _dot(sq_hi, pavg) + _dot(sq_lo, pavg)
            o_ref[0, :, j * LANES:(j + 1) * LANES] = (
                (ps * lax.rsqrt(m + EPS)) * gain_ref[...]).astype(BF16)

    v_ref[0] = proj(2).astype(BF16)

    xbuf[SUBLANES:SUBLANES + TM_IN, :] = proj(3)
    u = cb_ref[...]
    for kk in range(CONV_WIDTH):
        off = SUBLANES - (CONV_WIDTH - 1) + kk
        u = u + xbuf[off:off + TM_IN, :] * cw_ref[kk:kk + 1, :]
    xbuf[0:SUBLANES, :] = xbuf[TM_IN:TM_IN + SUBLANES, :]

    ub = u.astype(BF16)
    lam = lam_ref[...]
    sp_lam = _softplus(-lam)
    for j in range(LRU_WIDTH // LANES):
        cols = slice(j * LANES, (j + 1) * LANES)
        uj = ub[:, cols]
        r = jax.nn.sigmoid(_dot(uj, wrg_ref[j]) + brg_ref[:, cols])
        ig = jax.nn.sigmoid(_dot(uj, wig_ref[j]) + big_ref[:, cols])
        log_a = (-LRU_C * r) * sp_lam[:, cols]
        a = jnp.exp(log_a)
        a_s[j] = a
        mult = jnp.sqrt(-jnp.tanh(log_a) * (1.0 + a * a))
        b_s[j] = mult * (ig * u[:, cols])

    n_lane_tiles = LRU_WIDTH // LANES

    def scan_body(j, carry):
        out = []
        for c in range(n_lane_tiles):
            hh, pp = carry[c]
            aj = a_s[c, pl.ds(j, SUBLANES, stride=SEG), :]
            bj = b_s[c, pl.ds(j, SUBLANES, stride=SEG), :]
            hh = aj * hh + bj
            pp = aj * pp
            h_s[c, pl.ds(j, SUBLANES, stride=SEG), :] = hh
            p_s[c, pl.ds(j, SUBLANES, stride=SEG), :] = pp
            out.append((hh, pp))
        return tuple(out)

    init = tuple((jnp.zeros((SUBLANES, LANES), F32), jnp.ones((SUBLANES, LANES), F32))
                 for _ in range(n_lane_tiles))
    ends = lax.fori_loop(0, SEG, scan_body, init)

    gl = proj(4)
    for c in range(n_lane_tiles):
        cols = slice(c * LANES, (c + 1) * LANES)
        h_end, p_end = ends[c]
        st = state_s[0:1, cols]
        cin = []
        for i in range(SUBLANES):
            cin.append(st)
            st = p_end[i:i + 1, :] * st + h_end[i:i + 1, :]
        state_s[0:1, cols] = st
        for i in range(SUBLANES):
            rows = slice(i * SEG, (i + 1) * SEG)
            h = p_s[c, rows, :] * cin[i] + h_s[c, rows, :]
            lru_ref[0, rows, cols] = (h * jax.nn.gelu(gl[rows, cols])).astype(BF16)


def _mixer_in(x, g1, w_in, qg, kg, pavg, conv_w, conv_b, wrg, brg, wig, big, lam):
    B, S, _ = x.shape
    const2 = lambda b, t: (0, 0)
    const3 = lambda b, t: (0, 0, 0)
    tok = lambda b, t: (b, t, 0)
    out_sd = jax.ShapeDtypeStruct((B, S, ATTN_WIDTH), BF16)
    return pl.pallas_call(
        _mixer_in_kernel,
        grid=(B, S // TM_IN),
        in_specs=[
            pl.BlockSpec((1, TM_IN, D_MODEL), tok),
            pl.BlockSpec((1, D_MODEL), const2),
            pl.BlockSpec((D_MODEL, IN_WIDTH), const2),
            pl.BlockSpec((1, LANES), const2),
            pl.BlockSpec((1, LANES), const2),
            pl.BlockSpec((LANES, LANES), const2),
            pl.BlockSpec((CONV_WIDTH, LRU_WIDTH), const2),
            pl.BlockSpec((1, LRU_WIDTH), const2),
            pl.BlockSpec((LRU_WIDTH // LANES, LANES, LANES), const3),
            pl.BlockSpec((1, LRU_WIDTH), const2),
            pl.BlockSpec((LRU_WIDTH // LANES, LANES, LANES), const3),
            pl.BlockSpec((1, LRU_WIDTH), const2),
            pl.BlockSpec((1, LRU_WIDTH), const2),
        ],
        out_specs=[pl.BlockSpec((1, TM_IN, ATTN_WIDTH), tok)] * 4,
        out_shape=[out_sd] * 4,
        scratch_shapes=[
            pltpu.VMEM((TM_IN + SUBLANES, LRU_WIDTH), F32),
            pltpu.VMEM((LRU_WIDTH // LANES, TM_IN, LANES), F32),
            pltpu.VMEM((LRU_WIDTH // LANES, TM_IN, LANES), F32),
            pltpu.VMEM((LRU_WIDTH // LANES, TM_IN, LANES), F32),
            pltpu.VMEM((LRU_WIDTH // LANES, TM_IN, LANES), F32),
            pltpu.VMEM((SUBLANES, LRU_WIDTH), F32),
        ],
        compiler_params=pltpu.CompilerParams(
            dimension_semantics=("arbitrary", "arbitrary"),
            vmem_limit_bytes=VMEM_LIMIT),
        name="mixer_in",
    )(x, g1, w_in, qg, kg, pavg, conv_w, conv_b, wrg, brg, wig, big, lam)


def _sb_attn_kernel(q_ref, k_ref, v_ref, u_ref, o_ref, qs_ref, carry_ref, acc_ref):
    i = pl.program_id(2)
    lane = lax.broadcasted_iota(jnp.int32, (1, LANES), 1)
    q2 = q_ref[0].astype(F32)
    qs_ref[0:TQ, :] = jnp.where(lane < HEAD_DIM, q2, 0.0).astype(BF16)
    qs_ref[TQ:2 * TQ, :] = jnp.where(lane >= HEAD_DIM, q2, 0.0).astype(BF16)
    carry_ref[...] = jnp.zeros_like(carry_ref)
    acc_ref[...] = jnp.zeros_like(acc_ref)
    n_chunks = (2 * TQ) // ROW_CHUNK

    def step(j, masked):
        start = pl.multiple_of(j * TK, TK)
        kblk = k_ref[0, pl.ds(start, TK), :]
        vblk = v_ref[0, pl.ds(start, TK), :]
        z = _dot_nt(qs_ref[...], kblk)
        xr, xl, zc = [], [], []
        for c in range(n_chunks):
            rows = slice(c * ROW_CHUNK, (c + 1) * ROW_CHUNK)
            zz = z[rows, :]
            sp = _softplus(zz)
            if masked:
                qrow = (c * ROW_CHUNK) % TQ + lax.broadcasted_iota(jnp.int32, (ROW_CHUNK, TK), 0)
                kcol = lax.broadcasted_iota(jnp.int32, (ROW_CHUNK, TK), 1)
                keep = kcol < qrow
                sp = jnp.where(keep, sp, 0.0)
                zc.append((zz, keep))
            else:
                zc.append((zz, None))
            hi = lax.bitcast_convert_type(
                lax.bitcast_convert_type(sp, jnp.uint32) & jnp.uint32(0xFFFF0000), F32)
            lo = sp - hi
            hi = hi.astype(BF16)
            lo = lo.astype(BF16)
            xr.append(jnp.concatenate([hi[:, HALF:], lo[:, HALF:]], axis=1))
            xl.append(jnp.concatenate([hi[:, :HALF], lo[:, :HALF]], axis=1))
        xx = jnp.concatenate(xr + xl, axis=0)
        rr = _dot(xx, u_ref[...])
        ws = []
        for c in range(n_chunks):
            rows = slice(c * ROW_CHUNK, (c + 1) * ROW_CHUNK)
            rows_l = slice(2 * TQ + c * ROW_CHUNK, 2 * TQ + (c + 1) * ROW_CHUNK)
            zz, keep = zc[c]
            carry = carry_ref[rows, :]
            lw_r = zz[:, HALF:] + rr[rows, :HALF] + carry
            carry = carry + rr[rows, HALF:]
            lw_l = zz[:, :HALF] + rr[rows_l, :HALF] + carry
            carry_ref[rows, :] = carry + rr[rows_l, HALF:]
            w = jnp.exp(jnp.concatenate([lw_l, lw_r], axis=1))
            if masked:
                w = jnp.where(keep, w, 0.0)
            ws.append(w.astype(BF16))
        ww = jnp.concatenate(ws, axis=0)
        acc_ref[...] += _dot(ww, vblk)

    step(i, True)

    def body(jj, _):
        step(i - 1 - jj, False)
        return 0

    lax.fori_loop(0, i, body, 0)

    o_ref[0] = jnp.where(lane < HEAD_DIM, acc_ref[0:TQ, :], acc_ref[TQ:2 * TQ, :]).astype(BF16)


def _sb_attn(q, k, v, umat):
    B, S, _ = q.shape
    n_pairs = ATTN_WIDTH // LANES
    return pl.pallas_call(
        _sb_attn_kernel,
        grid=(B, n_pairs, S // TQ),
        in_specs=[
            pl.BlockSpec((1, TQ, LANES), lambda b, p, i: (b, i, p)),
            pl.BlockSpec((1, S, LANES), lambda b, p, i: (b, 0, p)),
            pl.BlockSpec((1, S, LANES), lambda b, p, i: (b, 0, p)),
            pl.BlockSpec((TK, TK), lambda b, p, i: (0, 0)),
        ],
        out_specs=pl.BlockSpec((1, TQ, LANES), lambda b, p, i: (b, i, p)),
        out_shape=jax.ShapeDtypeStruct((B, S, ATTN_WIDTH), BF16),
        scratch_shapes=[
            pltpu.VMEM((2 * TQ, LANES), BF16),
            pltpu.VMEM((2 * TQ, LANES), F32),
            pltpu.VMEM((2 * TQ, LANES), F32),
        ],
        compiler_params=pltpu.CompilerParams(
            dimension_semantics=("arbitrary", "arbitrary", "arbitrary"),
            vmem_limit_bytes=VMEM_LIMIT),
        name="sb_attn",
    )(q, k, v, umat)


def _out_mlp_kernel(x_ref, attn_ref, lru_ref, wo_ref, g2_ref, wup_ref, wdn_ref, o_ref, h_s):
    @pl.when(pl.program_id(1) == 0)
    def _():
        x1 = (x_ref[...] + _dot(attn_ref[...], wo_ref[0:ATTN_WIDTH, :])
              + _dot(lru_ref[...], wo_ref[ATTN_WIDTH:, :]))
        ms = jnp.mean(x1 * x1, axis=-1, keepdims=True)
        h_s[...] = ((x1 * lax.rsqrt(ms + EPS)) * g2_ref[...]).astype(BF16)
        o_ref[...] = x1

    up = jnp.maximum(_dot(h_s[...], wup_ref[...]), 0.0)
    o_ref[...] += _dot((up * up).astype(BF16), wdn_ref[...])


def _out_mlp(x, attn, lru, w_out, g2, w_up, w_down):
    n_tok = x.shape[0]
    tok = lambda t, f: (t, 0)
    const = lambda t, f: (0, 0)
    return pl.pallas_call(
        _out_mlp_kernel,
        grid=(n_tok // TM_MLP, D_FF // FF_CHUNK),
        in_specs=[
            pl.BlockSpec((TM_MLP, D_MODEL), tok),
            pl.BlockSpec((TM_MLP, ATTN_WIDTH), tok),
            pl.BlockSpec((TM_MLP, LRU_WIDTH), tok),
            pl.BlockSpec((D_MODEL, D_MODEL), const),
            pl.BlockSpec((1, D_MODEL), const),
            pl.BlockSpec((D_MODEL, FF_CHUNK), lambda t, f: (0, f)),
            pl.BlockSpec((FF_CHUNK, D_MODEL), lambda t, f: (f, 0)),
        ],
        out_specs=pl.BlockSpec((TM_MLP, D_MODEL), tok),
        out_shape=jax.ShapeDtypeStruct((n_tok, D_MODEL), F32),
        scratch_shapes=[pltpu.VMEM((TM_MLP, D_MODEL), BF16)],
        compiler_params=pltpu.CompilerParams(
            dimension_semantics=("arbitrary", "arbitrary"),
            vmem_limit_bytes=VMEM_LIMIT),
        name="out_mlp",
    )(x, attn, lru, w_out, g2, w_up, w_down)


def _pair_block_diag(w):
    n = w.shape[0] // 2
    w = w.reshape(n, 2, LRU_BLOCK, LRU_BLOCK)
    z = jnp.zeros((n, LRU_BLOCK, LRU_BLOCK), w.dtype)
    top = jnp.concatenate([w[:, 0], z], axis=2)
    bot = jnp.concatenate([z, w[:, 1]], axis=2)
    return jnp.concatenate([top, bot], axis=1)


def _suffix_matrix():
    j = jnp.arange(TK)[:, None] % HALF
    s = jnp.arange(TK)[None, :]
    return jnp.where((s >= HALF) | (j >= s), -1.0, 0.0).astype(BF16)


def kernel(x, norm1_g, w_in, conv_w, conv_b, w_rg, b_rg, w_ig, b_ig, lru_lambda, q_norm_g,
           k_norm_g, w_out, norm2_g, w_up, w_down):
    B, S, D = x.shape
    depth = w_in.shape[0]
    head = jnp.arange(LANES) // HEAD_DIM
    pavg = jnp.where(head[:, None] == head[None, :], 1.0 / HEAD_DIM, 0.0).astype(BF16)
    umat = _suffix_matrix()
    scale = 1.0 / math.sqrt(HEAD_DIM)
    row = lambda a: a.reshape(1, -1).astype(F32)
    for l in range(depth):
        qg = row(jnp.tile(q_norm_g[l], LANES // HEAD_DIM) * scale)
        kg = row(jnp.tile(k_norm_g[l], LANES // HEAD_DIM))
        q, k, v, lru = _mixer_in(
            x, row(norm1_g[l]), w_in[l].astype(BF16), qg, kg, pavg, conv_w[l], row(conv_b[l]),
            _pair_block_diag(w_rg[l]).astype(BF16), row(b_rg[l]),
            _pair_block_diag(w_ig[l]).astype(BF16), row(b_ig[l]), row(lru_lambda[l]))
        attn = _sb_attn(q, k, v, umat)
        x = _out_mlp(
            x.reshape(B * S, D), attn.reshape(B * S, ATTN_WIDTH), lru.reshape(B * S, LRU_WIDTH),
            w_out[l].astype(BF16), row(norm2_g[l]), w_up[l].astype(BF16),
            w_down[l].astype(BF16)).reshape(B, S, D)
    return x
```

```python
import functools
import math

import jax
import jax.numpy as jnp
from jax import lax
from jax.experimental import pallas as pl
from jax.experimental.pallas import tpu as pltpu

D_MODEL = 1024
HEAD_DIM = 64
ATTN_WIDTH = 512
LRU_WIDTH = 512
N_LRU_BLOCKS = 8
LRU_BLOCK = 64
IN_WIDTH = 3 * ATTN_WIDTH + 2 * LRU_WIDTH
CONV_WIDTH = 4
LRU_C = 8.0
D_FF = 4 * D_MODEL
EPS = 1e-6

LANES = 128
SUBLANES = 8
VMEM_LIMIT = 56 * 1024 * 1024

TM_IN = 512
SEG = TM_IN // SUBLANES
TM_MLP = 1024
FF_CHUNK = 1024
TQ = 256
TK = 256
HALF = TK // 2
ROW_CHUNK = 64
LOG_W_UNDERFLOW = -90.0

F32 = jnp.float32
BF16 = jnp.bfloat16


def _dot(a, b):
    return jnp.dot(a, b, preferred_element_type=F32)


def _dot_nt(a, b):
    return lax.dot_general(a, b, (((1,), (1,)), ((), ())), preferred_element_type=F32)


def _softplus(x):
    return jnp.maximum(x, 0.0) + jnp.log(1.0 + jnp.exp(-jnp.abs(x)))


def _mixer_in_kernel(x_ref, g1_ref, win_ref, qg_ref, kg_ref, pavg_ref, cw_ref, cb_ref,
                     wrg_ref, brg_ref, wig_ref, big_ref, lam_ref,
                     q_ref, k_ref, v_ref, lru_ref,
                     xbuf, a_s, b_s, p_s, h_s, state_s):
    t = pl.program_id(1)

    @pl.when(t == 0)
    def _():
        xbuf[0:SUBLANES, :] = jnp.zeros((SUBLANES, LRU_WIDTH), F32)
        state_s[...] = jnp.zeros_like(state_s)

    x = x_ref[0]
    ms = jnp.mean(x * x, axis=-1, keepdims=True)
    y = ((x * lax.rsqrt(ms + EPS)) * g1_ref[...]).astype(BF16)

    def proj(c):
        return _dot(y, win_ref[:, c * ATTN_WIDTH:(c + 1) * ATTN_WIDTH])

    pavg = pavg_ref[...]
    for c, (gain_ref, o_ref) in enumerate(((qg_ref, q_ref), (kg_ref, k_ref))):
        pr = proj(c)
        for j in range(ATTN_WIDTH // LANES):
            ps = pr[:, j * LANES:(j + 1) * LANES]
            sq = ps * ps
            sq_hi = sq.astype(BF16)
            sq_lo = (sq - sq_hi.astype(F32)).astype(BF16)
            m = _dot(sq_hi, pavg) + _dot(sq_lo, pavg)
            o_ref[0, :, j * LANES:(j + 1) * LANES] = (
                (ps * lax.rsqrt(m + EPS)) * gain_ref[...]).astype(BF16)

    v_ref[0] = proj(2).astype(BF16)

    xbuf[SUBLANES:SUBLANES + TM_IN, :] = proj(3)
    u = cb_ref[...]
    for kk in range(CONV_WIDTH):
        off = SUBLANES - (CONV_WIDTH - 1) + kk
        u = u + xbuf[off:off + TM_IN, :] * cw_ref[kk:kk + 1, :]
    xbuf[0:SUBLANES, :] = xbuf[TM_IN:TM_IN + SUBLANES, :]

    ub = u.astype(BF16)
    lam = lam_ref[...]
    sp_lam = _softplus(-lam)
    for j in range(LRU_WIDTH // LANES):
        cols = slice(j * LANES, (j + 1) * LANES)
        uj = ub[:, cols]
        r = jax.nn.sigmoid(_dot(uj, wrg_ref[j]) + brg_ref[:, cols])
        ig = jax.nn.sigmoid(_dot(uj, wig_ref[j]) + big_ref[:, cols])
        log_a = (-LRU_C * r) * sp_lam[:, cols]
        a = jnp.exp(log_a)
        a_s[j] = a
        mult = jnp.sqrt(-jnp.tanh(log_a) * (1.0 + a * a))
        b_s[j] = mult * (ig * u[:, cols])

    n_lane_tiles = LRU_WIDTH // LANES

    def scan_body(j, carry):
        out = []
        for c in range(n_lane_tiles):
            hh, pp = carry[c]
            aj = a_s[c, pl.ds(j, SUBLANES, stride=SEG), :]
            bj = b_s[c, pl.ds(j, SUBLANES, stride=SEG), :]
            hh = aj * hh + bj
            pp = aj * pp
            h_s[c, pl.ds(j, SUBLANES, stride=SEG), :] = hh
            p_s[c, pl.ds(j, SUBLANES, stride=SEG), :] = pp
            out.append((hh, pp))
        return tuple(out)

    init = tuple((jnp.zeros((SUBLANES, LANES), F32), jnp.ones((SUBLANES, LANES), F32))
                 for _ in range(n_lane_tiles))
    ends = lax.fori_loop(0, SEG, scan_body, init)

    gl = proj(4)
    for c in range(n_lane_tiles):
        cols = slice(c * LANES, (c + 1) * LANES)
        h_end, p_end = ends[c]
        st = state_s[0:1, cols]
        cin = []
        for i in range(SUBLANES):
            cin.append(st)
            st = p_end[i:i + 1, :] * st + h_end[i:i + 1, :]
        state_s[0:1, cols] = st
        for i in range(SUBLANES):
            rows = slice(i * SEG, (i + 1) * SEG)
            h = p_s[c, rows, :] * cin[i] + h_s[c, rows, :]
            lru_ref[0, rows, cols] = (h * jax.nn.gelu(gl[rows, cols])).astype(BF16)


def _mixer_in(x, g1, w_in, qg, kg, pavg, conv_w, conv_b, wrg, brg, wig, big, lam):
    B, S, _ = x.shape
    const2 = lambda b, t: (0, 0)
    const3 = lambda b, t: (0, 0, 0)
    tok = lambda b, t: (b, t, 0)
    out_sd = jax.ShapeDtypeStruct((B, S, ATTN_WIDTH), BF16)
    return pl.pallas_call(
        _mixer_in_kernel,
        grid=(B, S // TM_IN),
        in_specs=[
            pl.BlockSpec((1, TM_IN, D_MODEL), tok),
            pl.BlockSpec((1, D_MODEL), const2),
            pl.BlockSpec((D_MODEL, IN_WIDTH), const2),
            pl.BlockSpec((1, LANES), const2),
            pl.BlockSpec((1, LANES), const2),
            pl.BlockSpec((LANES, LANES), const2),
            pl.BlockSpec((CONV_WIDTH, LRU_WIDTH), const2),
            pl.BlockSpec((1, LRU_WIDTH), const2),
            pl.BlockSpec((LRU_WIDTH // LANES, LANES, LANES), const3),
            pl.BlockSpec((1, LRU_WIDTH), const2),
            pl.BlockSpec((LRU_WIDTH // LANES, LANES, LANES), const3),
            pl.BlockSpec((1, LRU_WIDTH), const2),
            pl.BlockSpec((1, LRU_WIDTH), const2),
        ],
        out_specs=[pl.BlockSpec((1, TM_IN, ATTN_WIDTH), tok)] * 4,
        out_shape=[out_sd] * 4,
        scratch_shapes=[
            pltpu.VMEM((TM_IN + SUBLANES, LRU_WIDTH), F32),
            pltpu.VMEM((LRU_WIDTH // LANES, TM_IN, LANES), F32),
            pltpu.VMEM((LRU_WIDTH // LANES, TM_IN, LANES), F32),
            pltpu.VMEM((LRU_WIDTH // LANES, TM_IN, LANES), F32),
            pltpu.VMEM((LRU_WIDTH // LANES, TM_IN, LANES), F32),
            pltpu.VMEM((SUBLANES, LRU_WIDTH), F32),
        ],
        compiler_params=pltpu.CompilerParams(
            dimension_semantics=("arbitrary", "arbitrary"),
            vmem_limit_bytes=VMEM_LIMIT),
        name="mixer_in",
    )(x, g1, w_in, qg, kg, pavg, conv_w, conv_b, wrg, brg, wig, big, lam)


def _sb_attn_kernel(q_ref, k_ref, v_ref, u_ref, o_ref, qs_ref, carry_ref, acc_ref):
    i = pl.program_id(2)
    lane = lax.broadcasted_iota(jnp.int32, (1, LANES), 1)
    q2 = q_ref[0].astype(F32)
    qs_ref[0:TQ, :] = jnp.where(lane < HEAD_DIM, q2, 0.0).astype(BF16)
    qs_ref[TQ:2 * TQ, :] = jnp.where(lane >= HEAD_DIM, q2, 0.0).astype(BF16)
    n_chunks = (2 * TQ) // ROW_CHUNK
    col_minus_row = (lax.broadcasted_iota(jnp.int32, (ROW_CHUNK, HALF), 1)
                     - lax.broadcasted_iota(jnp.int32, (ROW_CHUNK, HALF), 0))

    def step(start, nb, diag):
        nh = 2 * nb
        kblk = k_ref[0, pl.ds(start, nb * TK), :]
        vblk = v_ref[0, pl.ds(start, nb * TK), :]
        z = _dot_nt(qs_ref[...], kblk)

        def piece_mask(c, h):
            if not diag or h < nh - 2:
                return None
            q0 = (c * ROW_CHUNK) % TQ
            k0 = (h - (nh - 2)) * HALF
            if k0 + HALF - 1 < q0:
                return None
            if k0 >= q0 + ROW_CHUNK - 1:
                return False
            return col_minus_row < (q0 - k0)

        xs = [[None] * n_chunks for _ in range(nh)]
        for c in range(n_chunks):
            zz = z[c * ROW_CHUNK:(c + 1) * ROW_CHUNK, :]
            for h in range(nh):
                keep = piece_mask(c, h)
                if keep is False:
                    xs[h][c] = jnp.zeros((ROW_CHUNK, TK), BF16)
                    continue
                sp = _softplus(zz[:, h * HALF:(h + 1) * HALF])
                if keep is not None:
                    sp = jnp.where(keep, sp, 0.0)
                hi = lax.bitcast_convert_type(
                    lax.bitcast_convert_type(sp, jnp.uint32) & jnp.uint32(0xFFFF0000), F32)
                lo = sp - hi
                xs[h][c] = jnp.concatenate([hi.astype(BF16), lo.astype(BF16)], axis=1)
        xx = jnp.concatenate([xs[h][c] for h in reversed(range(nh)) for c in range(n_chunks)],
                             axis=0)
        rr = _dot(xx, u_ref[...])
        ws = []
        for c in range(n_chunks):
            rows = slice(c * ROW_CHUNK, (c + 1) * ROW_CHUNK)
            zz = z[rows, :]
            carry = None if diag else carry_ref[rows, :]
            wh = [None] * nh
            for idx, h in enumerate(reversed(range(nh))):
                keep = piece_mask(c, h)
                if keep is False:
                    wh[h] = jnp.zeros((ROW_CHUNK, HALF), F32)
                    continue
                base = idx * 2 * TQ + c * ROW_CHUNK
                lw = zz[:, h * HALF:(h + 1) * HALF] + rr[base:base + ROW_CHUNK, :HALF]
                tot = rr[base:base + ROW_CHUNK, HALF:]
                if carry is not None:
                    lw = lw + carry
                    tot = tot + carry
                carry = tot
                w = jnp.exp(lw)
                if keep is not None:
                    w = jnp.where(keep, w, 0.0)
                wh[h] = w
            carry_ref[rows, :] = carry
            ws.append(jnp.concatenate(wh, axis=1).astype(BF16))
        ww = jnp.concatenate(ws, axis=0)
        pv = _dot(ww, vblk)
        if diag:
            acc_ref[...] = pv
        else:
            acc_ref[...] += pv

    @pl.when(i == 0)
    def _():
        step(0, 1, True)

    @pl.when(i > 0)
    def _():
        step(pl.multiple_of((i - 1) * TK, TK), 2, True)

    def alive():
        return (jnp.max(carry_ref[...]) > LOG_W_UNDERFLOW).astype(jnp.int32)

    n_rest = jnp.maximum(i - 1, 0)

    def cond(state):
        jj, go = state
        return jnp.logical_and(jj < n_rest, go > 0)

    def body(state):
        jj, _ = state
        step(pl.multiple_of((n_rest - 1 - jj) * TK, TK), 1, False)
        return jj + 1, alive()

    lax.while_loop(cond, body, (jnp.int32(0), alive()))

    o_ref[0] = jnp.where(lane < HEAD_DIM, acc_ref[0:TQ, :], acc_ref[TQ:2 * TQ, :]).astype(BF16)


def _sb_attn(q, k, v, umat):
    B, S, _ = q.shape
    n_pairs = ATTN_WIDTH // LANES
    return pl.pallas_call(
        _sb_attn_kernel,
        grid=(B, n_pairs, S // TQ),
        in_specs=[
            pl.BlockSpec((1, TQ, LANES), lambda b, p, i: (b, i, p)),
            pl.BlockSpec((1, S, LANES), lambda b, p, i: (b, 0, p)),
            pl.BlockSpec((1, S, LANES), lambda b, p, i: (b, 0, p)),
            pl.BlockSpec((TK, TK), lambda b, p, i: (0, 0)),
        ],
        out_specs=pl.BlockSpec((1, TQ, LANES), lambda b, p, i: (b, i, p)),
        out_shape=jax.ShapeDtypeStruct((B, S, ATTN_WIDTH), BF16),
        scratch_shapes=[
            pltpu.VMEM((2 * TQ, LANES), BF16),
            pltpu.VMEM((2 * TQ, LANES), F32),
            pltpu.VMEM((2 * TQ, LANES), F32),
        ],
        compiler_params=pltpu.CompilerParams(
            dimension_semantics=("arbitrary", "arbitrary", "arbitrary"),
            vmem_limit_bytes=VMEM_LIMIT),
        name="sb_attn",
    )(q, k, v, umat)


def _out_mlp_kernel(x_ref, attn_ref, lru_ref, wo_ref, g2_ref, wup_ref, wdn_ref, o_ref, h_s):
    @pl.when(pl.program_id(1) == 0)
    def _():
        x1 = (x_ref[...] + _dot(attn_ref[...], wo_ref[0:ATTN_WIDTH, :])
              + _dot(lru_ref[...], wo_ref[ATTN_WIDTH:, :]))
        ms = jnp.mean(x1 * x1, axis=-1, keepdims=True)
        h_s[...] = ((x1 * lax.rsqrt(ms + EPS)) * g2_ref[...]).astype(BF16)
        o_ref[...] = x1

    up = jnp.maximum(_dot(h_s[...], wup_ref[...]), 0.0)
    o_ref[...] += _dot((up * up).astype(BF16), wdn_ref[...])


def _out_mlp(x, attn, lru, w_out, g2, w_up, w_down):
    n_tok = x.shape[0]
    tok = lambda t, f: (t, 0)
    const = lambda t, f: (0, 0)
    return pl.pallas_call(
        _out_mlp_kernel,
        grid=(n_tok // TM_MLP, D_FF // FF_CHUNK),
        in_specs=[
            pl.BlockSpec((TM_MLP, D_MODEL), tok),
            pl.BlockSpec((TM_MLP, ATTN_WIDTH), tok),
            pl.BlockSpec((TM_MLP, LRU_WIDTH), tok),
            pl.BlockSpec((D_MODEL, D_MODEL), const),
            pl.BlockSpec((1, D_MODEL), const),
            pl.BlockSpec((D_MODEL, FF_CHUNK), lambda t, f: (0, f)),
            pl.BlockSpec((FF_CHUNK, D_MODEL), lambda t, f: (f, 0)),
        ],
        out_specs=pl.BlockSpec((TM_MLP, D_MODEL), tok),
        out_shape=jax.ShapeDtypeStruct((n_tok, D_MODEL), F32),
        scratch_shapes=[pltpu.VMEM((TM_MLP, D_MODEL), BF16)],
        compiler_params=pltpu.CompilerParams(
            dimension_semantics=("arbitrary", "arbitrary"),
            vmem_limit_bytes=VMEM_LIMIT),
        name="out_mlp",
    )(x, attn, lru, w_out, g2, w_up, w_down)


def _pair_block_diag(w):
    n = w.shape[0] // 2
    w = w.reshape(n, 2, LRU_BLOCK, LRU_BLOCK)
    z = jnp.zeros((n, LRU_BLOCK, LRU_BLOCK), w.dtype)
    top = jnp.concatenate([w[:, 0], z], axis=2)
    bot = jnp.concatenate([z, w[:, 1]], axis=2)
    return jnp.concatenate([top, bot], axis=1)


def _suffix_matrix():
    j = jnp.arange(TK)[:, None] % HALF
    s = jnp.arange(TK)[None, :]
    return jnp.where((s >= HALF) | (j >= s), -1.0, 0.0).astype(BF16)


def kernel(x, norm1_g, w_in, conv_w, conv_b, w_rg, b_rg, w_ig, b_ig, lru_lambda, q_norm_g,
           k_norm_g, w_out, norm2_g, w_up, w_down):
    B, S, D = x.shape
    depth = w_in.shape[0]
    head = jnp.arange(LANES) // HEAD_DIM
    pavg = jnp.where(head[:, None] == head[None, :], 1.0 / HEAD_DIM, 0.0).astype(BF16)
    umat = _suffix_matrix()
    scale = 1.0 / math.sqrt(HEAD_DIM)
    row = lambda a: a.reshape(1, -1).astype(F32)
    for l in range(depth):
        qg = row(jnp.tile(q_norm_g[l], LANES // HEAD_DIM) * scale)
        kg = row(jnp.tile(k_norm_g[l], LANES // HEAD_DIM))
        q, k, v, lru = _mixer_in(
            x, row(norm1_g[l]), w_in[l].astype(BF16), qg, kg, pavg, conv_w[l], row(conv_b[l]),
            _pair_block_diag(w_rg[l]).astype(BF16), row(b_rg[l]),
            _pair_block_diag(w_ig[l]).astype(BF16), row(b_ig[l]), row(lru_lambda[l]))
        attn = _sb_attn(q, k, v, umat)
        x = _out_mlp(
            x.reshape(B * S, D), attn.reshape(B * S, ATTN_WIDTH), lru.reshape(B * S, LRU_WIDTH),
            w_out[l].astype(BF16), row(norm2_g[l]), w_up[l].astype(BF16),
            w_down[l].astype(BF16)).reshape(B, S, D)
    return x
```

```python
import functools
import math

import jax
import jax.numpy as jnp
from jax import lax
from jax.experimental import pallas as pl
from jax.experimental.pallas import tpu as pltpu

D_MODEL = 1024
HEAD_DIM = 64
ATTN_WIDTH = 512
LRU_WIDTH = 512
N_LRU_BLOCKS = 8
LRU_BLOCK = 64
IN_WIDTH = 3 * ATTN_WIDTH + 2 * LRU_WIDTH
CONV_WIDTH = 4
LRU_C = 8.0
D_FF = 4 * D_MODEL
EPS = 1e-6

LANES = 128
SUBLANES = 8
VMEM_LIMIT = 56 * 1024 * 1024

MXU_TILE = 256
TM_IN = 512
TM_MLP = 1024
FF_CHUNK = 1024
PAIRS = 2
TQ = 256
TK = 256
HALF = TK // 2
ROW_CHUNK = 64
GROUP_CHUNKS = 8
LOG2_W_UNDERFLOW = -130.0
LOG2_E = 1.4426950408889634

F32 = jnp.float32
BF16 = jnp.bfloat16


def _dot(a, b):
    return jnp.dot(a, b, preferred_element_type=F32)


def _dot_nt(a, b):
    return lax.dot_general(a, b, (((1,), (1,)), ((), ())), preferred_element_type=F32)


def _softplus(x):
    return jnp.maximum(x, 0.0) + jnp.log(1.0 + jnp.exp(-jnp.abs(x)))


def _mixer_in_kernel(x_ref, g1_ref, win_ref, qg_ref, kg_ref, pavg_ref, cw_ref, cb_ref,
                     wrg_ref, brg_ref, wig_ref, big_ref, lam_ref,
                     q_ref, k_ref, v_ref, lru_ref,
                     xbuf, state_s):
    t = pl.program_id(1)

    @pl.when(t == 0)
    def _():
        xbuf[0:SUBLANES, :] = jnp.zeros((SUBLANES, LRU_WIDTH), F32)
        state_s[...] = jnp.zeros_like(state_s)

    x = x_ref[0]
    ms = jnp.mean(x * x, axis=-1, keepdims=True)
    y = ((x * lax.rsqrt(ms + EPS)) * g1_ref[...]).astype(BF16)

    def proj(c):
        return _dot(y, win_ref[:, c * ATTN_WIDTH:(c + 1) * ATTN_WIDTH])

    pavg = pavg_ref[...]
    for c, (gain_ref, o_ref) in enumerate(((qg_ref, q_ref), (kg_ref, k_ref))):
        pr = proj(c)
        for j in range(ATTN_WIDTH // MXU_TILE):
            cols = slice(j * MXU_TILE, (j + 1) * MXU_TILE)
            ps = pr[:, cols]
            m = _dot((ps * ps).astype(BF16), pavg)
            o_ref[0, :, cols] = ((ps * lax.rsqrt(m + EPS)) * gain_ref[...]).astype(BF16)

    v_ref[0] = proj(2).astype(BF16)

    xbuf[SUBLANES:SUBLANES + TM_IN, :] = proj(3)
    u = cb_ref[...]
    for kk in range(CONV_WIDTH):
        off = SUBLANES - (CONV_WIDTH - 1) + kk
        u = u + xbuf[off:off + TM_IN, :] * cw_ref[kk:kk + 1, :]
    xbuf[0:SUBLANES, :] = xbuf[TM_IN:TM_IN + SUBLANES, :]

    ub = u.astype(BF16)
    lam = lam_ref[...]
    sp_lam = _softplus(-lam)
    gl = proj(4)
    n_vregs = TM_IN // SUBLANES
    sub = lax.broadcasted_iota(jnp.int32, (n_vregs, SUBLANES, LANES), 1)
    for j in range(LRU_WIDTH // MXU_TILE):
        wide = slice(j * MXU_TILE, (j + 1) * MXU_TILE)
        uj = ub[:, wide]
        r = jax.nn.sigmoid(_dot(uj, wrg_ref[j]) + brg_ref[:, wide])
        ig = jax.nn.sigmoid(_dot(uj, wig_ref[j]) + big_ref[:, wide])
        log_a = (-LRU_C * r) * sp_lam[:, wide]
        a_w = jnp.exp(log_a)
        mult = jnp.sqrt(-jnp.tanh(log_a) * (1.0 + a_w * a_w))
        b_w = mult * (ig * u[:, wide])

        for jj in range(MXU_TILE // LANES):
            cols = slice(j * MXU_TILE + jj * LANES, j * MXU_TILE + (jj + 1) * LANES)
            a = a_w[:, jj * LANES:(jj + 1) * LANES].reshape(n_vregs, SUBLANES, LANES)
            b = b_w[:, jj * LANES:(jj + 1) * LANES].reshape(n_vregs, SUBLANES, LANES)
            d = 1
            while d < SUBLANES:
                below = sub >= d
                b = jnp.where(below, b + a * pltpu.roll(b, d, axis=1), b)
                a = jnp.where(below, a * pltpu.roll(a, d, axis=1), a)
                d *= 2
            st = jnp.broadcast_to(state_s[0:1, cols], (SUBLANES, LANES))
            hs = []
            for m in range(n_vregs):
                h = a[m] * st + b[m]
                hs.append(h)
                st = jnp.broadcast_to(h[SUBLANES - 1:SUBLANES, :], (SUBLANES, LANES))
            state_s[0:1, cols] = st[0:1, :]
            h_all = jnp.concatenate(hs, axis=0)
            lru_ref[0, :, cols] = (h_all * jax.nn.gelu(gl[:, cols])).astype(BF16)


def _mixer_in(x, g1, w_in, qg, kg, pavg, conv_w, conv_b, wrg, brg, wig, big, lam):
    B, S, _ = x.shape
    const2 = lambda b, t: (0, 0)
    const3 = lambda b, t: (0, 0, 0)
    tok = lambda b, t: (b, t, 0)
    out_sd = jax.ShapeDtypeStruct((B, S, ATTN_WIDTH), BF16)
    return pl.pallas_call(
        _mixer_in_kernel,
        grid=(B, S // TM_IN),
        in_specs=[
            pl.BlockSpec((1, TM_IN, D_MODEL), tok),
            pl.BlockSpec((1, D_MODEL), const2),
            pl.BlockSpec((D_MODEL, IN_WIDTH), const2),
            pl.BlockSpec((1, MXU_TILE), const2),
            pl.BlockSpec((1, MXU_TILE), const2),
            pl.BlockSpec((MXU_TILE, MXU_TILE), const2),
            pl.BlockSpec((CONV_WIDTH, LRU_WIDTH), const2),
            pl.BlockSpec((1, LRU_WIDTH), const2),
            pl.BlockSpec((LRU_WIDTH // MXU_TILE, MXU_TILE, MXU_TILE), const3),
            pl.BlockSpec((1, LRU_WIDTH), const2),
            pl.BlockSpec((LRU_WIDTH // MXU_TILE, MXU_TILE, MXU_TILE), const3),
            pl.BlockSpec((1, LRU_WIDTH), const2),
            pl.BlockSpec((1, LRU_WIDTH), const2),
        ],
        out_specs=[pl.BlockSpec((1, TM_IN, ATTN_WIDTH), tok)] * 4,
        out_shape=[out_sd] * 4,
        scratch_shapes=[
            pltpu.VMEM((TM_IN + SUBLANES, LRU_WIDTH), F32),
            pltpu.VMEM((SUBLANES, LRU_WIDTH), F32),
        ],
        compiler_params=pltpu.CompilerParams(
            dimension_semantics=("arbitrary", "arbitrary"),
            vmem_limit_bytes=VMEM_LIMIT),
        name="mixer_in",
    )(x, g1, w_in, qg, kg, pavg, conv_w, conv_b, wrg, brg, wig, big, lam)


def _sb_attn_kernel(q_ref, k_ref, v_ref, u_ref, o_ref, qs_ref, carry_ref, acc_ref, go_ref):
    i = pl.program_id(2)
    lane = lax.broadcasted_iota(jnp.int32, (1, LANES), 1)
    pair_rows = 2 * TQ
    for p in range(PAIRS):
        q2 = q_ref[0, :, p * LANES:(p + 1) * LANES].astype(F32)
        r0 = p * pair_rows
        qs_ref[r0:r0 + TQ, :] = jnp.where(lane < HEAD_DIM, q2, 0.0).astype(BF16)
        qs_ref[r0 + TQ:r0 + pair_rows, :] = jnp.where(lane >= HEAD_DIM, q2, 0.0).astype(BF16)
    n_chunks = (PAIRS * pair_rows) // ROW_CHUNK
    col_minus_row = (lax.broadcasted_iota(jnp.int32, (ROW_CHUNK, HALF), 1)
                     - lax.broadcasted_iota(jnp.int32, (ROW_CHUNK, HALF), 0))

    def step(start, nb, diag):
        nh = 2 * nb
        kblk = [k_ref[0, pl.ds(start, nb * TK), p * LANES:(p + 1) * LANES]
                for p in range(PAIRS)]
        vblk = [v_ref[0, pl.ds(start, nb * TK), p * LANES:(p + 1) * LANES]
                for p in range(PAIRS)]
        grp_rows = GROUP_CHUNKS * ROW_CHUNK
        zg = [_dot_nt(qs_ref[g * grp_rows:(g + 1) * grp_rows, :],
                      kblk[g * grp_rows // pair_rows])
              for g in range(PAIRS * pair_rows // grp_rows)]

        def zpiece(c, h):
            r0 = (c % GROUP_CHUNKS) * ROW_CHUNK
            return zg[c // GROUP_CHUNKS][r0:r0 + ROW_CHUNK, h * HALF:(h + 1) * HALF]

        def piece_mask(c, h):
            if not diag or h < nh - 2:
                return None
            q0 = (c * ROW_CHUNK) % TQ
            k0 = (h - (nh - 2)) * HALF
            if k0 + HALF - 1 < q0:
                return None
            if k0 >= q0 + ROW_CHUNK - 1:
                return False
            return col_minus_row < (q0 - k0)

        def suffix_sums(chunks):
            pieces, offset = [], {}
            for b in reversed(range(nb)):
                for c in chunks:
                    sps = []
                    for h in (2 * b, 2 * b + 1):
                        keep = piece_mask(c, h)
                        if keep is False:
                            sps.append(jnp.zeros((ROW_CHUNK, HALF), BF16))
                            continue
                        zp = zpiece(c, h)
                        neg_abs = lax.bitcast_convert_type(
                            lax.bitcast_convert_type(zp, jnp.uint32) | jnp.uint32(0x80000000), F32)
                        sp = jnp.maximum(zp, 0.0) + jnp.log2(1.0 + jnp.exp2(neg_abs))
                        if keep is not None:
                            sp = jnp.where(keep, sp, 0.0)
                        sps.append(sp.astype(BF16))
                    offset[(b, c)] = len(pieces) * ROW_CHUNK
                    pieces.append(jnp.concatenate(sps, axis=1))
            xx = jnp.concatenate(pieces, axis=0)
            return _dot(xx, u_ref[...]), offset

        def weights(chunks, rr, offset):
            ws, carries = [], []
            for c in chunks:
                rows = slice(c * ROW_CHUNK, (c + 1) * ROW_CHUNK)
                carry = None if diag else carry_ref[rows, :]
                wh = [None] * nh
                for b in reversed(range(nb)):
                    base = offset[(b, c)]
                    suf = rr[base:base + ROW_CHUNK, :]
                    tot = jnp.broadcast_to(suf[:, 0:1], (ROW_CHUNK, HALF))
                    for h in (2 * b + 1, 2 * b):
                        keep = piece_mask(c, h)
                        if keep is False:
                            wh[h] = jnp.zeros((ROW_CHUNK, HALF), F32)
                            continue
                        lw = zpiece(c, h) + suf[:, (h % 2) * HALF:(h % 2 + 1) * HALF]
                        if carry is not None:
                            lw = lw + carry
                        w = jnp.exp2(lw)
                        if keep is not None:
                            w = jnp.where(keep, w, 0.0)
                        wh[h] = w
                    carry = tot if carry is None else carry + tot
                carry_ref[rows, :] = carry
                carries.append(carry)
                ws.append(jnp.concatenate(wh, axis=1).astype(BF16))
            return jnp.concatenate(ws, axis=0), carries

        def finish(chunks, ww):
            rows = slice(chunks[0] * ROW_CHUNK, (chunks[-1] + 1) * ROW_CHUNK)
            pv = _dot(ww, vblk[chunks[0] * ROW_CHUNK // pair_rows])
            if diag:
                acc_ref[rows, :] = pv
            else:
                acc_ref[rows, :] += pv

        groups = [list(range(g, g + GROUP_CHUNKS)) for g in range(0, n_chunks, GROUP_CHUNKS)]
        all_carries, sums = [], {}
        for k in range(len(groups) + 1):
            if k < len(groups):
                sums[k] = suffix_sums(groups[k])
            if k >= 1:
                ww, carries = weights(groups[k - 1], *sums.pop(k - 1))
                all_carries += carries
                if k == len(groups):
                    cmax = functools.reduce(jnp.maximum, all_carries)
                    go_ref[0] = (jnp.max(cmax) > LOG2_W_UNDERFLOW).astype(jnp.int32)
                finish(groups[k - 1], ww)

    @pl.when(i == 0)
    def _():
        step(0, 1, True)

    @pl.when(i > 0)
    def _():
        step(pl.multiple_of((i - 1) * TK, TK), 2, True)

    n_rest = jnp.maximum(i - 1, 0)

    def cond(state):
        jj, go = state
        return jnp.logical_and(jj < n_rest, go > 0)

    def body(state):
        jj, _ = state
        step(pl.multiple_of((n_rest - 1 - jj) * TK, TK), 1, False)
        return jj + 1, go_ref[0]

    lax.while_loop(cond, body, (jnp.int32(0), go_ref[0]))

    for p in range(PAIRS):
        r0 = p * pair_rows
        o_ref[0, :, p * LANES:(p + 1) * LANES] = jnp.where(
            lane < HEAD_DIM, acc_ref[r0:r0 + TQ, :], acc_ref[r0 + TQ:r0 + pair_rows, :]).astype(BF16)


def _sb_attn(q, k, v, umat):
    B, S, _ = q.shape
    width = PAIRS * LANES
    stacked = PAIRS * 2 * TQ
    return pl.pallas_call(
        _sb_attn_kernel,
        grid=(B, ATTN_WIDTH // width, S // TQ),
        in_specs=[
            pl.BlockSpec((1, TQ, width), lambda b, p, i: (b, i, p)),
            pl.BlockSpec((1, S, width), lambda b, p, i: (b, 0, p)),
            pl.BlockSpec((1, S, width), lambda b, p, i: (b, 0, p)),
            pl.BlockSpec((TK, TK), lambda b, p, i: (0, 0)),
        ],
        out_specs=pl.BlockSpec((1, TQ, width), lambda b, p, i: (b, i, p)),
        out_shape=jax.ShapeDtypeStruct((B, S, ATTN_WIDTH), BF16),
        scratch_shapes=[
            pltpu.VMEM((stacked, LANES), BF16),
            pltpu.VMEM((stacked, LANES), F32),
            pltpu.VMEM((stacked, LANES), F32),
            pltpu.SMEM((1,), jnp.int32),
        ],
        compiler_params=pltpu.CompilerParams(
            dimension_semantics=("arbitrary", "arbitrary", "arbitrary"),
            vmem_limit_bytes=VMEM_LIMIT),
        name="sb_attn",
    )(q, k, v, umat)


def _out_mlp_kernel(x_ref, attn_ref, lru_ref, wo_ref, g2_ref, wup_ref, wdn_ref, o_ref, h_s):
    @pl.when(pl.program_id(1) == 0)
    def _():
        x1 = (x_ref[...] + _dot(attn_ref[...], wo_ref[0:ATTN_WIDTH, :])
              + _dot(lru_ref[...], wo_ref[ATTN_WIDTH:, :]))
        ms = jnp.mean(x1 * x1, axis=-1, keepdims=True)
        h_s[...] = ((x1 * lax.rsqrt(ms + EPS)) * g2_ref[...]).astype(BF16)
        o_ref[...] = x1

    up = jnp.maximum(_dot(h_s[...], wup_ref[...]), 0.0)
    o_ref[...] += _dot((up * up).astype(BF16), wdn_ref[...])


def _out_mlp(x, attn, lru, w_out, g2, w_up, w_down):
    n_tok = x.shape[0]
    tok = lambda t, f: (t, 0)
    const = lambda t, f: (0, 0)
    return pl.pallas_call(
        _out_mlp_kernel,
        grid=(n_tok // TM_MLP, D_FF // FF_CHUNK),
        in_specs=[
            pl.BlockSpec((TM_MLP, D_MODEL), tok),
            pl.BlockSpec((TM_MLP, ATTN_WIDTH), tok),
            pl.BlockSpec((TM_MLP, LRU_WIDTH), tok),
            pl.BlockSpec((D_MODEL, D_MODEL), const),
            pl.BlockSpec((1, D_MODEL), const),
            pl.BlockSpec((D_MODEL, FF_CHUNK), lambda t, f: (0, f)),
            pl.BlockSpec((FF_CHUNK, D_MODEL), lambda t, f: (f, 0)),
        ],
        out_specs=pl.BlockSpec((TM_MLP, D_MODEL), tok),
        out_shape=jax.ShapeDtypeStruct((n_tok, D_MODEL), F32),
        scratch_shapes=[pltpu.VMEM((TM_MLP, D_MODEL), BF16)],
        compiler_params=pltpu.CompilerParams(
            dimension_semantics=("arbitrary", "arbitrary"),
            vmem_limit_bytes=VMEM_LIMIT),
        name="out_mlp",
    )(x, attn, lru, w_out, g2, w_up, w_down)


def _packed_block_diag(w):
    per = MXU_TILE // LRU_BLOCK
    w = w.reshape(w.shape[0] // per, per, LRU_BLOCK, LRU_BLOCK)
    eye = jnp.eye(per, dtype=w.dtype)
    out = w[:, :, :, None, :] * eye[None, :, None, :, None]
    return out.reshape(w.shape[0], MXU_TILE, MXU_TILE)


def _suffix_matrix():
    j = jnp.arange(TK)[:, None]
    s = jnp.arange(TK)[None, :]
    return jnp.where(j >= s, -1.0, 0.0).astype(BF16)


def kernel(x, norm1_g, w_in, conv_w, conv_b, w_rg, b_rg, w_ig, b_ig, lru_lambda, q_norm_g,
           k_norm_g, w_out, norm2_g, w_up, w_down):
    B, S, D = x.shape
    depth = w_in.shape[0]
    head = jnp.arange(MXU_TILE) // HEAD_DIM
    pavg = jnp.where(head[:, None] == head[None, :], 1.0 / HEAD_DIM, 0.0).astype(BF16)
    umat = _suffix_matrix()
    scale = LOG2_E / math.sqrt(HEAD_DIM)
    row = lambda a: a.reshape(1, -1).astype(F32)
    for l in range(depth):
        qg = row(jnp.tile(q_norm_g[l], MXU_TILE // HEAD_DIM) * scale)
        kg = row(jnp.tile(k_norm_g[l], MXU_TILE // HEAD_DIM))
        q, k, v, lru = _mixer_in(
            x, row(norm1_g[l]), w_in[l].astype(BF16), qg, kg, pavg, conv_w[l], row(conv_b[l]),
            _packed_block_diag(w_rg[l]).astype(BF16), row(b_rg[l]),
            _packed_block_diag(w_ig[l]).astype(BF16), row(b_ig[l]), row(lru_lambda[l]))
        attn = _sb_attn(q, k, v, umat)
        x = _out_mlp(
            x.reshape(B * S, D), attn.reshape(B * S, ATTN_WIDTH), lru.reshape(B * S, LRU_WIDTH),
            w_out[l].astype(BF16), row(norm2_g[l]), w_up[l].astype(BF16),
            w_down[l].astype(BF16)).reshape(B, S, D)
    return x
```

```python
import functools
import math

import jax
import jax.numpy as jnp
from jax import lax
from jax.experimental import pallas as pl
from jax.experimental.pallas import tpu as pltpu

D_MODEL = 1024
HEAD_DIM = 64
ATTN_WIDTH = 512
LRU_WIDTH = 512
N_LRU_BLOCKS = 8
LRU_BLOCK = 64
IN_WIDTH = 3 * ATTN_WIDTH + 2 * LRU_WIDTH
CONV_WIDTH = 4
LRU_C = 8.0
D_FF = 4 * D_MODEL
EPS = 1e-6

LANES = 128
SUBLANES = 8
VMEM_LIMIT = 56 * 1024 * 1024

MXU_TILE = 256
TM_IN = 512
TM_MLP = 1024
FF_CHUNK = 1024
PAIRS = 2
TQ = 256
TK = 256
HALF = TK // 2
ROW_CHUNK = 64
GROUP_CHUNKS = 8
LOG2_W_UNDERFLOW = -130.0
LOG2_E = 1.4426950408889634

F32 = jnp.float32
BF16 = jnp.bfloat16


def _dot(a, b):
    return jnp.dot(a, b, preferred_element_type=F32)


def _dot_nt(a, b):
    return lax.dot_general(a, b, (((1,), (1,)), ((), ())), preferred_element_type=F32)


def _softplus(x):
    return jnp.maximum(x, 0.0) + jnp.log(1.0 + jnp.exp(-jnp.abs(x)))


def _mixer_in_kernel(x_ref, g1_ref, win_ref, qg_ref, kg_ref, pavg_ref, cw_ref, cb_ref,
                     wrg_ref, brg_ref, wig_ref, big_ref, lam_ref,
                     q_ref, k_ref, v_ref, lru_ref,
                     xbuf, state_s):
    t = pl.program_id(1)

    @pl.when(t == 0)
    def _():
        xbuf[0:SUBLANES, :] = jnp.zeros((SUBLANES, LRU_WIDTH), F32)
        state_s[...] = jnp.zeros_like(state_s)

    x = x_ref[0]
    ms = jnp.mean(x * x, axis=-1, keepdims=True)
    y = ((x * lax.rsqrt(ms + EPS)) * g1_ref[...]).astype(BF16)

    def proj(col0, width):
        return _dot(y, win_ref[:, col0:col0 + width])

    n_vregs = TM_IN // SUBLANES
    sub = lax.broadcasted_iota(jnp.int32, (n_vregs, SUBLANES, LANES), 1)
    lam = lam_ref[...]
    sp_lam = _softplus(-lam)
    pavg = pavg_ref[...]

    def attn_chunk(c, j):
        cols = slice(j * MXU_TILE, (j + 1) * MXU_TILE)
        ps = proj(c * ATTN_WIDTH + j * MXU_TILE, MXU_TILE)
        if c == 2:
            v_ref[0, :, cols] = ps.astype(BF16)
            return
        gain_ref, o_ref = ((qg_ref, q_ref), (kg_ref, k_ref))[c]
        m = _dot((ps * ps).astype(BF16), pavg)
        o_ref[0, :, cols] = ((ps * lax.rsqrt(m + EPS)) * gain_ref[...]).astype(BF16)

    def conv(j):
        wide = slice(j * MXU_TILE, (j + 1) * MXU_TILE)
        xbuf[SUBLANES:SUBLANES + TM_IN, wide] = proj(3 * ATTN_WIDTH + j * MXU_TILE, MXU_TILE)
        u = cb_ref[:, wide]
        for kk in range(CONV_WIDTH):
            off = SUBLANES - (CONV_WIDTH - 1) + kk
            u = u + xbuf[off:off + TM_IN, wide] * cw_ref[kk:kk + 1, wide]
        xbuf[0:SUBLANES, wide] = xbuf[TM_IN:TM_IN + SUBLANES, wide]
        return u

    def gates(u, j):
        wide = slice(j * MXU_TILE, (j + 1) * MXU_TILE)
        uj = u.astype(BF16)
        r = jax.nn.sigmoid(_dot(uj, wrg_ref[j]) + brg_ref[:, wide])
        ig = jax.nn.sigmoid(_dot(uj, wig_ref[j]) + big_ref[:, wide])
        log_a = (-LRU_C * r) * sp_lam[:, wide]
        a_w = jnp.exp(log_a)
        mult = jnp.sqrt(-jnp.tanh(log_a) * (1.0 + a_w * a_w))
        return a_w, mult * (ig * u)

    def scan(a_w, b_w, gl, j):
        tiles = range(MXU_TILE // LANES)
        ab = []
        for jj in tiles:
            a = a_w[:, jj * LANES:(jj + 1) * LANES].reshape(n_vregs, SUBLANES, LANES)
            b = b_w[:, jj * LANES:(jj + 1) * LANES].reshape(n_vregs, SUBLANES, LANES)
            d = 1
            while d < SUBLANES:
                below = sub >= d
                b = jnp.where(below, b + a * pltpu.roll(b, d, axis=1), b)
                a = jnp.where(below, a * pltpu.roll(a, d, axis=1), a)
                d *= 2
            ab.append((a, b))
        cols = [slice(j * MXU_TILE + jj * LANES, j * MXU_TILE + (jj + 1) * LANES) for jj in tiles]
        st = [jnp.broadcast_to(state_s[0:1, cols[jj]], (SUBLANES, LANES)) for jj in tiles]
        hs = [[] for _ in tiles]
        for m in range(n_vregs):
            for jj in tiles:
                h = ab[jj][0][m] * st[jj] + ab[jj][1][m]
                hs[jj].append(h)
                st[jj] = jnp.broadcast_to(h[SUBLANES - 1:SUBLANES, :], (SUBLANES, LANES))
        for jj in tiles:
            state_s[0:1, cols[jj]] = st[jj][0:1, :]
            h_all = jnp.concatenate(hs[jj], axis=0)
            g = gl[:, jj * LANES:(jj + 1) * LANES]
            lru_ref[0, :, cols[jj]] = (h_all * jax.nn.gelu(g)).astype(BF16)

    u0 = conv(0)
    u1 = conv(1)
    attn_chunk(0, 0)
    ab0 = gates(u0, 0)
    gl0 = proj(4 * ATTN_WIDTH, MXU_TILE)
    attn_chunk(0, 1)
    scan(*ab0, gl0, 0)
    attn_chunk(1, 0)
    ab1 = gates(u1, 1)
    gl1 = proj(4 * ATTN_WIDTH + MXU_TILE, MXU_TILE)
    attn_chunk(2, 0)
    attn_chunk(2, 1)
    scan(*ab1, gl1, 1)
    attn_chunk(1, 1)


def _mixer_in(x, g1, w_in, qg, kg, pavg, conv_w, conv_b, wrg, brg, wig, big, lam):
    B, S, _ = x.shape
    const2 = lambda b, t: (0, 0)
    const3 = lambda b, t: (0, 0, 0)
    tok = lambda b, t: (b, t, 0)
    out_sd = jax.ShapeDtypeStruct((B, S, ATTN_WIDTH), BF16)
    return pl.pallas_call(
        _mixer_in_kernel,
        grid=(B, S // TM_IN),
        in_specs=[
            pl.BlockSpec((1, TM_IN, D_MODEL), tok),
            pl.BlockSpec((1, D_MODEL), const2),
            pl.BlockSpec((D_MODEL, IN_WIDTH), const2),
            pl.BlockSpec((1, MXU_TILE), const2),
            pl.BlockSpec((1, MXU_TILE), const2),
            pl.BlockSpec((MXU_TILE, MXU_TILE), const2),
            pl.BlockSpec((CONV_WIDTH, LRU_WIDTH), const2),
            pl.BlockSpec((1, LRU_WIDTH), const2),
            pl.BlockSpec((LRU_WIDTH // MXU_TILE, MXU_TILE, MXU_TILE), const3),
            pl.BlockSpec((1, LRU_WIDTH), const2),
            pl.BlockSpec((LRU_WIDTH // MXU_TILE, MXU_TILE, MXU_TILE), const3),
            pl.BlockSpec((1, LRU_WIDTH), const2),
            pl.BlockSpec((1, LRU_WIDTH), const2),
        ],
        out_specs=[pl.BlockSpec((1, TM_IN, ATTN_WIDTH), tok)] * 4,
        out_shape=[out_sd] * 4,
        scratch_shapes=[
            pltpu.VMEM((TM_IN + SUBLANES, LRU_WIDTH), F32),
            pltpu.VMEM((SUBLANES, LRU_WIDTH), F32),
        ],
        compiler_params=pltpu.CompilerParams(
            dimension_semantics=("arbitrary", "arbitrary"),
            vmem_limit_bytes=VMEM_LIMIT),
        name="mixer_in",
    )(x, g1, w_in, qg, kg, pavg, conv_w, conv_b, wrg, brg, wig, big, lam)


def _sb_attn_kernel(q_ref, k_ref, v_ref, u_ref, o_ref, qs_ref, carry_ref, acc_ref, go_ref):
    i = pl.program_id(2)
    lane = lax.broadcasted_iota(jnp.int32, (1, LANES), 1)
    pair_rows = 2 * TQ
    for p in range(PAIRS):
        q2 = q_ref[0, :, p * LANES:(p + 1) * LANES].astype(F32)
        r0 = p * pair_rows
        qs_ref[r0:r0 + TQ, :] = jnp.where(lane < HEAD_DIM, q2, 0.0).astype(BF16)
        qs_ref[r0 + TQ:r0 + pair_rows, :] = jnp.where(lane >= HEAD_DIM, q2, 0.0).astype(BF16)
    n_chunks = (PAIRS * pair_rows) // ROW_CHUNK
    col_minus_row = (lax.broadcasted_iota(jnp.int32, (ROW_CHUNK, HALF), 1)
                     - lax.broadcasted_iota(jnp.int32, (ROW_CHUNK, HALF), 0))

    def step(start, nb, diag):
        nh = 2 * nb
        kblk = [k_ref[0, pl.ds(start, nb * TK), p * LANES:(p + 1) * LANES]
                for p in range(PAIRS)]
        vblk = [v_ref[0, pl.ds(start, nb * TK), p * LANES:(p + 1) * LANES]
                for p in range(PAIRS)]
        grp_rows = GROUP_CHUNKS * ROW_CHUNK
        zg = [_dot_nt(qs_ref[g * grp_rows:(g + 1) * grp_rows, :],
                      kblk[g * grp_rows // pair_rows])
              for g in range(PAIRS * pair_rows // grp_rows)]

        def zpiece(c, h):
            r0 = (c % GROUP_CHUNKS) * ROW_CHUNK
            return zg[c // GROUP_CHUNKS][r0:r0 + ROW_CHUNK, h * HALF:(h + 1) * HALF]

        def piece_mask(c, h):
            if not diag or h < nh - 2:
                return None
            q0 = (c * ROW_CHUNK) % TQ
            k0 = (h - (nh - 2)) * HALF
            if k0 + HALF - 1 < q0:
                return None
            if k0 >= q0 + ROW_CHUNK - 1:
                return False
            return col_minus_row < (q0 - k0)

        def suffix_sums(chunks):
            pieces, offset = [], {}
            for b in reversed(range(nb)):
                for c in chunks:
                    sps = []
                    for h in (2 * b, 2 * b + 1):
                        keep = piece_mask(c, h)
                        if keep is False:
                            sps.append(jnp.zeros((ROW_CHUNK, HALF), BF16))
                            continue
                        zp = zpiece(c, h)
                        pos = jnp.maximum(zp, 0.0)
                        neg_abs = (zp - pos) - pos
                        sp = pos + jnp.log2(1.0 + jnp.exp2(neg_abs))
                        if keep is not None:
                            sp = jnp.where(keep, sp, 0.0)
                        sps.append(sp.astype(BF16))
                    offset[(b, c)] = len(pieces) * ROW_CHUNK
                    pieces.append(jnp.concatenate(sps, axis=1))
            xx = jnp.concatenate(pieces, axis=0)
            return _dot(xx, u_ref[...]), offset

        def weights(chunks, rr, offset):
            ws, carries = [], []
            for c in chunks:
                rows = slice(c * ROW_CHUNK, (c + 1) * ROW_CHUNK)
                carry = None if diag else carry_ref[rows, :]
                wh = [None] * nh
                for b in reversed(range(nb)):
                    base = offset[(b, c)]
                    suf = rr[base:base + ROW_CHUNK, :]
                    tot = jnp.broadcast_to(suf[:, 0:1], (ROW_CHUNK, HALF))
                    for h in (2 * b + 1, 2 * b):
                        keep = piece_mask(c, h)
                        if keep is False:
                            wh[h] = jnp.zeros((ROW_CHUNK, HALF), F32)
                            continue
                        lw = zpiece(c, h) + suf[:, (h % 2) * HALF:(h % 2 + 1) * HALF]
                        if carry is not None:
                            lw = lw + carry
                        w = jnp.exp2(lw)
                        if keep is not None:
                            w = jnp.where(keep, w, 0.0)
                        wh[h] = w
                    carry = tot if carry is None else carry + tot
                carry_ref[rows, :] = carry
                carries.append(carry)
                ws.append(jnp.concatenate(wh, axis=1).astype(BF16))
            return jnp.concatenate(ws, axis=0), carries

        def finish(chunks, ww):
            rows = slice(chunks[0] * ROW_CHUNK, (chunks[-1] + 1) * ROW_CHUNK)
            pv = _dot(ww, vblk[chunks[0] * ROW_CHUNK // pair_rows])
            if diag:
                acc_ref[rows, :] = pv
            else:
                acc_ref[rows, :] += pv

        groups = [list(range(g, g + GROUP_CHUNKS)) for g in range(0, n_chunks, GROUP_CHUNKS)]
        all_carries, sums = [], {}
        for k in range(len(groups) + 1):
            if k < len(groups):
                sums[k] = suffix_sums(groups[k])
            if k >= 1:
                ww, carries = weights(groups[k - 1], *sums.pop(k - 1))
                all_carries += carries
                if k == len(groups):
                    cmax = functools.reduce(jnp.maximum, all_carries)
                    go_ref[0] = (jnp.max(cmax) > LOG2_W_UNDERFLOW).astype(jnp.int32)
                finish(groups[k - 1], ww)

    @pl.when(i == 0)
    def _():
        step(0, 1, True)

    @pl.when(i > 0)
    def _():
        step(pl.multiple_of((i - 1) * TK, TK), 2, True)

    n_rest = jnp.maximum(i - 1, 0)

    def cond(state):
        jj, go = state
        return jnp.logical_and(jj < n_rest, go > 0)

    def body(state):
        jj, _ = state
        step(pl.multiple_of((n_rest - 1 - jj) * TK, TK), 1, False)
        return jj + 1, go_ref[0]

    lax.while_loop(cond, body, (jnp.int32(0), go_ref[0]))

    for p in range(PAIRS):
        r0 = p * pair_rows
        o_ref[0, :, p * LANES:(p + 1) * LANES] = jnp.where(
            lane < HEAD_DIM, acc_ref[r0:r0 + TQ, :], acc_ref[r0 + TQ:r0 + pair_rows, :]).astype(BF16)


def _sb_attn(q, k, v, umat):
    B, S, _ = q.shape
    width = PAIRS * LANES
    stacked = PAIRS * 2 * TQ
    return pl.pallas_call(
        _sb_attn_kernel,
        grid=(B, ATTN_WIDTH // width, S // TQ),
        in_specs=[
            pl.BlockSpec((1, TQ, width), lambda b, p, i: (b, i, p)),
            pl.BlockSpec((1, S, width), lambda b, p, i: (b, 0, p)),
            pl.BlockSpec((1, S, width), lambda b, p, i: (b, 0, p)),
            pl.BlockSpec((TK, TK), lambda b, p, i: (0, 0)),
        ],
        out_specs=pl.BlockSpec((1, TQ, width), lambda b, p, i: (b, i, p)),
        out_shape=jax.ShapeDtypeStruct((B, S, ATTN_WIDTH), BF16),
        scratch_shapes=[
            pltpu.VMEM((stacked, LANES), BF16),
            pltpu.VMEM((stacked, LANES), F32),
            pltpu.VMEM((stacked, LANES), F32),
            pltpu.SMEM((1,), jnp.int32),
        ],
        compiler_params=pltpu.CompilerParams(
            dimension_semantics=("arbitrary", "arbitrary", "arbitrary"),
            vmem_limit_bytes=VMEM_LIMIT),
        name="sb_attn",
    )(q, k, v, umat)


def _out_mlp_kernel(x_ref, attn_ref, lru_ref, wo_ref, g2_ref, wup_ref, wdn_ref, o_ref, h_s):
    @pl.when(pl.program_id(1) == 0)
    def _():
        x1 = (x_ref[...] + _dot(attn_ref[...], wo_ref[0:ATTN_WIDTH, :])
              + _dot(lru_ref[...], wo_ref[ATTN_WIDTH:, :]))
        ms = jnp.mean(x1 * x1, axis=-1, keepdims=True)
        h_s[...] = ((x1 * lax.rsqrt(ms + EPS)) * g2_ref[...]).astype(BF16)
        o_ref[...] = x1

    up = jnp.maximum(_dot(h_s[...], wup_ref[...]), 0.0)
    o_ref[...] += _dot((up * up).astype(BF16), wdn_ref[...])


def _out_mlp(x, attn, lru, w_out, g2, w_up, w_down):
    n_tok = x.shape[0]
    tok = lambda t, f: (t, 0)
    const = lambda t, f: (0, 0)
    return pl.pallas_call(
        _out_mlp_kernel,
        grid=(n_tok // TM_MLP, D_FF // FF_CHUNK),
        in_specs=[
            pl.BlockSpec((TM_MLP, D_MODEL), tok),
            pl.BlockSpec((TM_MLP, ATTN_WIDTH), tok),
            pl.BlockSpec((TM_MLP, LRU_WIDTH), tok),
            pl.BlockSpec((D_MODEL, D_MODEL), const),
            pl.BlockSpec((1, D_MODEL), const),
            pl.BlockSpec((D_MODEL, FF_CHUNK), lambda t, f: (0, f)),
            pl.BlockSpec((FF_CHUNK, D_MODEL), lambda t, f: (f, 0)),
        ],
        out_specs=pl.BlockSpec((TM_MLP, D_MODEL), tok),
        out_shape=jax.ShapeDtypeStruct((n_tok, D_MODEL), F32),
        scratch_shapes=[pltpu.VMEM((TM_MLP, D_MODEL), BF16)],
        compiler_params=pltpu.CompilerParams(
            dimension_semantics=("arbitrary", "arbitrary"),
            vmem_limit_bytes=VMEM_LIMIT),
        name="out_mlp",
    )(x, attn, lru, w_out, g2, w_up, w_down)


def _packed_block_diag(w):
    per = MXU_TILE // LRU_BLOCK
    w = w.reshape(w.shape[0] // per, per, LRU_BLOCK, LRU_BLOCK)
    eye = jnp.eye(per, dtype=w.dtype)
    out = w[:, :, :, None, :] * eye[None, :, None, :, None]
    return out.reshape(w.shape[0], MXU_TILE, MXU_TILE)


def _suffix_matrix():
    j = jnp.arange(TK)[:, None]
    s = jnp.arange(TK)[None, :]
    return jnp.where(j >= s, -1.0, 0.0).astype(BF16)


def kernel(x, norm1_g, w_in, conv_w, conv_b, w_rg, b_rg, w_ig, b_ig, lru_lambda, q_norm_g,
           k_norm_g, w_out, norm2_g, w_up, w_down):
    B, S, D = x.shape
    depth = w_in.shape[0]
    head = jnp.arange(MXU_TILE) // HEAD_DIM
    pavg = jnp.where(head[:, None] == head[None, :], 1.0 / HEAD_DIM, 0.0).astype(BF16)
    umat = _suffix_matrix()
    scale = LOG2_E / math.sqrt(HEAD_DIM)
    row = lambda a: a.reshape(1, -1).astype(F32)
    for l in range(depth):
        qg = row(jnp.tile(q_norm_g[l], MXU_TILE // HEAD_DIM) * scale)
        kg = row(jnp.tile(k_norm_g[l], MXU_TILE // HEAD_DIM))
        q, k, v, lru = _mixer_in(
            x, row(norm1_g[l]), w_in[l].astype(BF16), qg, kg, pavg, conv_w[l], row(conv_b[l]),
            _packed_block_diag(w_rg[l]).astype(BF16), row(b_rg[l]),
            _packed_block_diag(w_ig[l]).astype(BF16), row(b_ig[l]), row(lru_lambda[l]))
        attn = _sb_attn(q, k, v, umat)
        x = _out_mlp(
            x.reshape(B * S, D), attn.reshape(B * S, ATTN_WIDTH), lru.reshape(B * S, LRU_WIDTH),
            w_out[l].astype(BF16), row(norm2_g[l]), w_up[l].astype(BF16),
            w_down[l].astype(BF16)).reshape(B, S, D)
    return x
```

```python
import functools
import math

import jax
import jax.numpy as jnp
from jax import lax
from jax.experimental import pallas as pl
from jax.experimental.pallas import tpu as pltpu

D_MODEL = 1024
HEAD_DIM = 64
ATTN_WIDTH = 512
LRU_WIDTH = 512
N_LRU_BLOCKS = 8
LRU_BLOCK = 64
IN_WIDTH = 3 * ATTN_WIDTH + 2 * LRU_WIDTH
CONV_WIDTH = 4
LRU_C = 8.0
D_FF = 4 * D_MODEL
EPS = 1e-6

LANES = 128
SUBLANES = 8
VMEM_LIMIT = 56 * 1024 * 1024

MXU_TILE = 256
TM_IN = 512
TM_MLP = 1024
FF_CHUNK = 1024
PAIRS = 4
TQ = 256
TK = 256
HALF = TK // 2
ROW_CHUNK = 64
GROUP_CHUNKS = 8
LOG2_W_UNDERFLOW = -130.0
LOG2_E = 1.4426950408889634

F32 = jnp.float32
BF16 = jnp.bfloat16


def _dot(a, b):
    return jnp.dot(a, b, preferred_element_type=F32)


def _dot_nt(a, b):
    return lax.dot_general(a, b, (((1,), (1,)), ((), ())), preferred_element_type=F32)


def _softplus(x):
    return jnp.maximum(x, 0.0) + jnp.log(1.0 + jnp.exp(-jnp.abs(x)))


def _mixer_in_kernel(x_ref, g1_ref, win_ref, qg_ref, kg_ref, pavg_ref, cw_ref, cb_ref,
                     wrg_ref, brg_ref, wig_ref, big_ref, lam_ref,
                     q_ref, k_ref, v_ref, lru_ref,
                     xbuf, state_s):
    t = pl.program_id(1)

    @pl.when(t == 0)
    def _():
        xbuf[0:SUBLANES, :] = jnp.zeros((SUBLANES, LRU_WIDTH), F32)
        state_s[...] = jnp.zeros_like(state_s)

    x = x_ref[0]
    ms = jnp.mean(x * x, axis=-1, keepdims=True)
    y = ((x * lax.rsqrt(ms + EPS)) * g1_ref[...]).astype(BF16)

    def proj(col0, width):
        return _dot(y, win_ref[:, col0:col0 + width])

    n_vregs = TM_IN // SUBLANES
    sub = lax.broadcasted_iota(jnp.int32, (n_vregs, SUBLANES, LANES), 1)
    lam = lam_ref[...]
    sp_lam = _softplus(-lam)
    pavg = pavg_ref[...]

    def attn_chunk(c, j):
        cols = slice(j * MXU_TILE, (j + 1) * MXU_TILE)
        ps = proj(c * ATTN_WIDTH + j * MXU_TILE, MXU_TILE)
        if c == 2:
            v_ref[0, :, cols] = ps.astype(BF16)
            return
        gain_ref, o_ref = ((qg_ref, q_ref), (kg_ref, k_ref))[c]
        m = _dot((ps * ps).astype(BF16), pavg)
        o_ref[0, :, cols] = ((ps * lax.rsqrt(m + EPS)) * gain_ref[...]).astype(BF16)

    def conv(j):
        wide = slice(j * MXU_TILE, (j + 1) * MXU_TILE)
        xbuf[SUBLANES:SUBLANES + TM_IN, wide] = proj(3 * ATTN_WIDTH + j * MXU_TILE, MXU_TILE)
        u = cb_ref[:, wide]
        for kk in range(CONV_WIDTH):
            off = SUBLANES - (CONV_WIDTH - 1) + kk
            u = u + xbuf[off:off + TM_IN, wide] * cw_ref[kk:kk + 1, wide]
        xbuf[0:SUBLANES, wide] = xbuf[TM_IN:TM_IN + SUBLANES, wide]
        return u

    def gates(u, j):
        wide = slice(j * MXU_TILE, (j + 1) * MXU_TILE)
        uj = u.astype(BF16)
        r = jax.nn.sigmoid(_dot(uj, wrg_ref[j]) + brg_ref[:, wide])
        ig = jax.nn.sigmoid(_dot(uj, wig_ref[j]) + big_ref[:, wide])
        log_a = (-LRU_C * r) * sp_lam[:, wide]
        a_w = jnp.exp(log_a)
        mult = jnp.sqrt(-jnp.tanh(log_a) * (1.0 + a_w * a_w))
        return a_w, mult * (ig * u)

    def scan(a_w, b_w, gl, j):
        tiles = range(MXU_TILE // LANES)
        ab = []
        for jj in tiles:
            a = a_w[:, jj * LANES:(jj + 1) * LANES].reshape(n_vregs, SUBLANES, LANES)
            b = b_w[:, jj * LANES:(jj + 1) * LANES].reshape(n_vregs, SUBLANES, LANES)
            d = 1
            while d < SUBLANES:
                below = sub >= d
                b = jnp.where(below, b + a * pltpu.roll(b, d, axis=1), b)
                a = jnp.where(below, a * pltpu.roll(a, d, axis=1), a)
                d *= 2
            ab.append((a, b))
        cols = [slice(j * MXU_TILE + jj * LANES, j * MXU_TILE + (jj + 1) * LANES) for jj in tiles]
        st = [jnp.broadcast_to(state_s[0:1, cols[jj]], (SUBLANES, LANES)) for jj in tiles]
        hs = [[] for _ in tiles]
        for m in range(n_vregs):
            for jj in tiles:
                h = ab[jj][0][m] * st[jj] + ab[jj][1][m]
                hs[jj].append(h)
                st[jj] = jnp.broadcast_to(h[SUBLANES - 1:SUBLANES, :], (SUBLANES, LANES))
        for jj in tiles:
            state_s[0:1, cols[jj]] = st[jj][0:1, :]
            h_all = jnp.concatenate(hs[jj], axis=0)
            g = gl[:, jj * LANES:(jj + 1) * LANES]
            lru_ref[0, :, cols[jj]] = (h_all * jax.nn.gelu(g)).astype(BF16)

    u0 = conv(0)
    u1 = conv(1)
    attn_chunk(0, 0)
    ab0 = gates(u0, 0)
    gl0 = proj(4 * ATTN_WIDTH, MXU_TILE)
    attn_chunk(0, 1)
    scan(*ab0, gl0, 0)
    attn_chunk(1, 0)
    ab1 = gates(u1, 1)
    gl1 = proj(4 * ATTN_WIDTH + MXU_TILE, MXU_TILE)
    attn_chunk(2, 0)
    attn_chunk(2, 1)
    scan(*ab1, gl1, 1)
    attn_chunk(1, 1)


def _layer_spec(l, *shape):
    return pl.BlockSpec((None,) + shape, lambda *_: (l,) + (0,) * len(shape))


def _mixer_in(l, x, g1, w_in, qg, kg, pavg, conv_w, conv_b, wrg, brg, wig, big, lam):
    B, S, _ = x.shape
    tok = lambda b, t: (b, t, 0)
    out_sd = jax.ShapeDtypeStruct((B, S, ATTN_WIDTH), BF16)
    n_wide = LRU_WIDTH // MXU_TILE
    return pl.pallas_call(
        _mixer_in_kernel,
        grid=(B, S // TM_IN),
        in_specs=[
            pl.BlockSpec((1, TM_IN, D_MODEL), tok),
            _layer_spec(l, 1, D_MODEL),
            _layer_spec(l, D_MODEL, IN_WIDTH),
            _layer_spec(l, 1, MXU_TILE),
            _layer_spec(l, 1, MXU_TILE),
            pl.BlockSpec((MXU_TILE, MXU_TILE), lambda b, t: (0, 0)),
            _layer_spec(l, CONV_WIDTH, LRU_WIDTH),
            _layer_spec(l, 1, LRU_WIDTH),
            _layer_spec(l, n_wide, MXU_TILE, MXU_TILE),
            _layer_spec(l, 1, LRU_WIDTH),
            _layer_spec(l, n_wide, MXU_TILE, MXU_TILE),
            _layer_spec(l, 1, LRU_WIDTH),
            _layer_spec(l, 1, LRU_WIDTH),
        ],
        out_specs=[pl.BlockSpec((1, TM_IN, ATTN_WIDTH), tok)] * 4,
        out_shape=[out_sd] * 4,
        scratch_shapes=[
            pltpu.VMEM((TM_IN + SUBLANES, LRU_WIDTH), F32),
            pltpu.VMEM((SUBLANES, LRU_WIDTH), F32),
        ],
        compiler_params=pltpu.CompilerParams(
            dimension_semantics=("arbitrary", "arbitrary"),
            vmem_limit_bytes=VMEM_LIMIT),
        name="mixer_in",
    )(x, g1, w_in, qg, kg, pavg, conv_w, conv_b, wrg, brg, wig, big, lam)


def _sb_attn_kernel(q_ref, k_ref, v_ref, u_ref, o_ref, qs_ref, carry_ref, acc_ref, go_ref):
    i = pl.program_id(2)
    lane = lax.broadcasted_iota(jnp.int32, (1, LANES), 1)
    pair_rows = 2 * TQ
    for p in range(PAIRS):
        q2 = q_ref[0, :, p * LANES:(p + 1) * LANES].astype(F32)
        r0 = p * pair_rows
        qs_ref[r0:r0 + TQ, :] = jnp.where(lane < HEAD_DIM, q2, 0.0).astype(BF16)
        qs_ref[r0 + TQ:r0 + pair_rows, :] = jnp.where(lane >= HEAD_DIM, q2, 0.0).astype(BF16)
    n_chunks = (PAIRS * pair_rows) // ROW_CHUNK
    col_minus_row = (lax.broadcasted_iota(jnp.int32, (ROW_CHUNK, HALF), 1)
                     - lax.broadcasted_iota(jnp.int32, (ROW_CHUNK, HALF), 0))

    def step(start, nb, diag):
        nh = 2 * nb
        kblk = [k_ref[0, pl.ds(start, nb * TK), p * LANES:(p + 1) * LANES]
                for p in range(PAIRS)]
        vblk = [v_ref[0, pl.ds(start, nb * TK), p * LANES:(p + 1) * LANES]
                for p in range(PAIRS)]
        grp_rows = GROUP_CHUNKS * ROW_CHUNK
        zg = [_dot_nt(qs_ref[g * grp_rows:(g + 1) * grp_rows, :],
                      kblk[g * grp_rows // pair_rows])
              for g in range(PAIRS * pair_rows // grp_rows)]

        def zpiece(c, h):
            r0 = (c % GROUP_CHUNKS) * ROW_CHUNK
            return zg[c // GROUP_CHUNKS][r0:r0 + ROW_CHUNK, h * HALF:(h + 1) * HALF]

        def piece_mask(c, h):
            if not diag or h < nh - 2:
                return None
            q0 = (c * ROW_CHUNK) % TQ
            k0 = (h - (nh - 2)) * HALF
            if k0 + HALF - 1 < q0:
                return None
            if k0 >= q0 + ROW_CHUNK - 1:
                return False
            return col_minus_row < (q0 - k0)

        def suffix_sums(chunks):
            pieces, offset = [], {}
            for b in reversed(range(nb)):
                for c in chunks:
                    sps = []
                    for h in (2 * b, 2 * b + 1):
                        keep = piece_mask(c, h)
                        if keep is False:
                            sps.append(jnp.zeros((ROW_CHUNK, HALF), BF16))
                            continue
                        zp = zpiece(c, h)
                        pos = jnp.maximum(zp, 0.0)
                        neg_abs = (zp - pos) - pos
                        sp = pos + jnp.log2(1.0 + jnp.exp2(neg_abs))
                        if keep is not None:
                            sp = jnp.where(keep, sp, 0.0)
                        sps.append(sp.astype(BF16))
                    offset[(b, c)] = len(pieces) * ROW_CHUNK
                    pieces.append(jnp.concatenate(sps, axis=1))
            xx = jnp.concatenate(pieces, axis=0)
            return _dot(xx, u_ref[...]), offset

        def weights(chunks, rr, offset):
            ws, carries = [], []
            for c in chunks:
                rows = slice(c * ROW_CHUNK, (c + 1) * ROW_CHUNK)
                carry = None if diag else carry_ref[rows, :]
                wh = [None] * nh
                for b in reversed(range(nb)):
                    base = offset[(b, c)]
                    suf = rr[base:base + ROW_CHUNK, :]
                    tot = jnp.broadcast_to(suf[:, 0:1], (ROW_CHUNK, HALF))
                    for h in (2 * b + 1, 2 * b):
                        keep = piece_mask(c, h)
                        if keep is False:
                            wh[h] = jnp.zeros((ROW_CHUNK, HALF), F32)
                            continue
                        lw = zpiece(c, h) + suf[:, (h % 2) * HALF:(h % 2 + 1) * HALF]
                        if carry is not None:
                            lw = lw + carry
                        w = jnp.exp2(lw)
                        if keep is not None:
                            w = jnp.where(keep, w, 0.0)
                        wh[h] = w
                    carry = tot if carry is None else carry + tot
                carry_ref[rows, :] = carry
                carries.append(carry)
                ws.append(jnp.concatenate(wh, axis=1).astype(BF16))
            return jnp.concatenate(ws, axis=0), carries

        def finish(chunks, ww):
            rows = slice(chunks[0] * ROW_CHUNK, (chunks[-1] + 1) * ROW_CHUNK)
            pv = _dot(ww, vblk[chunks[0] * ROW_CHUNK // pair_rows])
            if diag:
                acc_ref[rows, :] = pv
            else:
                acc_ref[rows, :] += pv

        groups = [list(range(g, g + GROUP_CHUNKS)) for g in range(0, n_chunks, GROUP_CHUNKS)]
        all_carries, sums = [], {}
        for k in range(len(groups) + 1):
            if k < len(groups):
                sums[k] = suffix_sums(groups[k])
            if k >= 1:
                ww, carries = weights(groups[k - 1], *sums.pop(k - 1))
                all_carries += carries
                if k == len(groups):
                    cmax = functools.reduce(jnp.maximum, all_carries)
                    go_ref[0] = (jnp.max(cmax) > LOG2_W_UNDERFLOW).astype(jnp.int32)
                finish(groups[k - 1], ww)

    @pl.when(i == 0)
    def _():
        step(0, 1, True)

    @pl.when(i > 0)
    def _():
        step(pl.multiple_of((i - 1) * TK, TK), 2, True)

    n_rest = jnp.maximum(i - 1, 0)

    def cond(state):
        jj, go = state
        return jnp.logical_and(jj < n_rest, go > 0)

    def body(state):
        jj, _ = state
        step(pl.multiple_of((n_rest - 1 - jj) * TK, TK), 1, False)
        return jj + 1, go_ref[0]

    lax.while_loop(cond, body, (jnp.int32(0), go_ref[0]))

    for p in range(PAIRS):
        r0 = p * pair_rows
        o_ref[0, :, p * LANES:(p + 1) * LANES] = jnp.where(
            lane < HEAD_DIM, acc_ref[r0:r0 + TQ, :], acc_ref[r0 + TQ:r0 + pair_rows, :]).astype(BF16)


def _sb_attn(q, k, v, umat):
    B, S, _ = q.shape
    width = PAIRS * LANES
    stacked = PAIRS * 2 * TQ
    return pl.pallas_call(
        _sb_attn_kernel,
        grid=(B, ATTN_WIDTH // width, S // TQ),
        in_specs=[
            pl.BlockSpec((1, TQ, width), lambda b, p, i: (b, i, p)),
            pl.BlockSpec((1, S, width), lambda b, p, i: (b, 0, p)),
            pl.BlockSpec((1, S, width), lambda b, p, i: (b, 0, p)),
            pl.BlockSpec((TK, TK), lambda b, p, i: (0, 0)),
        ],
        out_specs=pl.BlockSpec((1, TQ, width), lambda b, p, i: (b, i, p)),
        out_shape=jax.ShapeDtypeStruct((B, S, ATTN_WIDTH), BF16),
        scratch_shapes=[
            pltpu.VMEM((stacked, LANES), BF16),
            pltpu.VMEM((stacked, LANES), F32),
            pltpu.VMEM((stacked, LANES), F32),
            pltpu.SMEM((1,), jnp.int32),
        ],
        compiler_params=pltpu.CompilerParams(
            dimension_semantics=("arbitrary", "arbitrary", "arbitrary"),
            vmem_limit_bytes=VMEM_LIMIT),
        name="sb_attn",
    )(q, k, v, umat)


def _out_mlp_kernel(x_ref, attn_ref, lru_ref, wo_ref, g2_ref, wup_ref, wdn_ref, o_ref, h_s):
    @pl.when(pl.program_id(1) == 0)
    def _():
        x1 = (x_ref[...] + _dot(attn_ref[...], wo_ref[0:ATTN_WIDTH, :])
              + _dot(lru_ref[...], wo_ref[ATTN_WIDTH:, :]))
        ms = jnp.mean(x1 * x1, axis=-1, keepdims=True)
        h_s[...] = ((x1 * lax.rsqrt(ms + EPS)) * g2_ref[...]).astype(BF16)
        o_ref[...] = x1

    up = jnp.maximum(_dot(h_s[...], wup_ref[...]), 0.0)
    o_ref[...] += _dot((up * up).astype(BF16), wdn_ref[...])


def _out_mlp(l, x, attn, lru, w_out, g2, w_up, w_down):
    n_tok = x.shape[0]
    tok = lambda t, f: (t, 0)
    return pl.pallas_call(
        _out_mlp_kernel,
        grid=(n_tok // TM_MLP, D_FF // FF_CHUNK),
        in_specs=[
            pl.BlockSpec((TM_MLP, D_MODEL), tok),
            pl.BlockSpec((TM_MLP, ATTN_WIDTH), tok),
            pl.BlockSpec((TM_MLP, LRU_WIDTH), tok),
            _layer_spec(l, D_MODEL, D_MODEL),
            _layer_spec(l, 1, D_MODEL),
            pl.BlockSpec((None, D_MODEL, FF_CHUNK), lambda t, f: (l, 0, f)),
            pl.BlockSpec((None, FF_CHUNK, D_MODEL), lambda t, f: (l, f, 0)),
        ],
        out_specs=pl.BlockSpec((TM_MLP, D_MODEL), tok),
        out_shape=jax.ShapeDtypeStruct((n_tok, D_MODEL), F32),
        scratch_shapes=[pltpu.VMEM((TM_MLP, D_MODEL), BF16)],
        compiler_params=pltpu.CompilerParams(
            dimension_semantics=("arbitrary", "arbitrary"),
            vmem_limit_bytes=VMEM_LIMIT),
        name="out_mlp",
    )(x, attn, lru, w_out, g2, w_up, w_down)


def _packed_block_diag(w):
    per = MXU_TILE // LRU_BLOCK
    depth, n_blocks = w.shape[:2]
    w = w.reshape(depth, n_blocks // per, per, LRU_BLOCK, LRU_BLOCK)
    eye = jnp.eye(per, dtype=w.dtype)
    out = w[:, :, :, :, None, :] * eye[None, None, :, None, :, None]
    return out.reshape(depth, n_blocks // per, MXU_TILE, MXU_TILE)


def _suffix_matrix():
    j = jnp.arange(TK)[:, None]
    s = jnp.arange(TK)[None, :]
    return jnp.where(j >= s, -1.0, 0.0).astype(BF16)


def kernel(x, norm1_g, w_in, conv_w, conv_b, w_rg, b_rg, w_ig, b_ig, lru_lambda, q_norm_g,
           k_norm_g, w_out, norm2_g, w_up, w_down):
    B, S, D = x.shape
    depth = w_in.shape[0]
    head = jnp.arange(MXU_TILE) // HEAD_DIM
    pavg = jnp.where(head[:, None] == head[None, :], 1.0 / HEAD_DIM, 0.0).astype(BF16)
    umat = _suffix_matrix()
    scale = LOG2_E / math.sqrt(HEAD_DIM)
    rows = lambda a: a.reshape(depth, 1, -1).astype(F32)
    per_tile = (1, MXU_TILE // HEAD_DIM)
    mixer_params = (
        rows(norm1_g), w_in.astype(BF16), rows(jnp.tile(q_norm_g, per_tile) * scale),
        rows(jnp.tile(k_norm_g, per_tile)), pavg, conv_w, rows(conv_b),
        _packed_block_diag(w_rg).astype(BF16), rows(b_rg),
        _packed_block_diag(w_ig).astype(BF16), rows(b_ig), rows(lru_lambda))
    mlp_params = (w_out.astype(BF16), rows(norm2_g), w_up.astype(BF16), w_down.astype(BF16))
    for l in range(depth):
        q, k, v, lru = _mixer_in(l, x, *mixer_params)
        attn = _sb_attn(q, k, v, umat)
        x = _out_mlp(
            l, x.reshape(B * S, D), attn.reshape(B * S, ATTN_WIDTH),
            lru.reshape(B * S, LRU_WIDTH), *mlp_params).reshape(B, S, D)
    return x
```

```python
import functools
import math

import jax
import jax.numpy as jnp
from jax import lax
from jax.experimental import pallas as pl
from jax.experimental.pallas import tpu as pltpu

D_MODEL = 1024
HEAD_DIM = 64
ATTN_WIDTH = 512
LRU_WIDTH = 512
N_LRU_BLOCKS = 8
LRU_BLOCK = 64
IN_WIDTH = 3 * ATTN_WIDTH + 2 * LRU_WIDTH
CONV_WIDTH = 4
LRU_C = 8.0
D_FF = 4 * D_MODEL
EPS = 1e-6

LANES = 128
SUBLANES = 8
VMEM_LIMIT = 56 * 1024 * 1024

MXU_TILE = 256
TM_IN = 512
TM_MLP = 1024
FF_CHUNK = 2048
FF_INNER = 1024
PAIRS = 4
TQ = 256
TK = 256
HALF = TK // 2
ROW_CHUNK = 64
GROUP_CHUNKS = 8
LOG2_W_UNDERFLOW = -130.0
LOG2_E = 1.4426950408889634

F32 = jnp.float32
BF16 = jnp.bfloat16


def _dot(a, b):
    return jnp.dot(a, b, preferred_element_type=F32)


def _dot_nt(a, b):
    return lax.dot_general(a, b, (((1,), (1,)), ((), ())), preferred_element_type=F32)


def _softplus(x):
    return jnp.maximum(x, 0.0) + jnp.log(1.0 + jnp.exp(-jnp.abs(x)))


def _mixer_in_kernel(x_ref, g1_ref, win_ref, qg_ref, kg_ref, pavg_ref, cw_ref, cb_ref,
                     wrg_ref, brg_ref, wig_ref, big_ref, lam_ref,
                     q_ref, k_ref, v_ref, lru_ref,
                     xbuf, state_s, *, layer):
    t = pl.program_id(1)
    row = slice(layer, layer + 1)
    g1, qg, kg, cb = g1_ref[row, :], qg_ref[row, :], kg_ref[row, :], cb_ref[row, :]
    brg, big, lam = brg_ref[row, :], big_ref[row, :], lam_ref[row, :]

    @pl.when(t == 0)
    def _():
        xbuf[0:SUBLANES, :] = jnp.zeros((SUBLANES, LRU_WIDTH), F32)
        state_s[...] = jnp.zeros_like(state_s)

    x = x_ref[0]
    ms = jnp.mean(x * x, axis=-1, keepdims=True)
    y = ((x * lax.rsqrt(ms + EPS)) * g1).astype(BF16)

    def proj(col0, width):
        return _dot(y, win_ref[:, col0:col0 + width])

    n_vregs = TM_IN // SUBLANES
    sub = lax.broadcasted_iota(jnp.int32, (n_vregs, SUBLANES, LANES), 1)
    sp_lam = _softplus(-lam)
    pavg = pavg_ref[...]

    def attn_chunk(c, j):
        cols = slice(j * MXU_TILE, (j + 1) * MXU_TILE)
        ps = proj(c * ATTN_WIDTH + j * MXU_TILE, MXU_TILE)
        if c == 2:
            v_ref[0, :, cols] = ps.astype(BF16)
            return
        gain, o_ref = ((qg, q_ref), (kg, k_ref))[c]
        m = _dot((ps * ps).astype(BF16), pavg)
        o_ref[0, :, cols] = ((ps * lax.rsqrt(m + EPS)) * gain).astype(BF16)

    def conv(j):
        wide = slice(j * MXU_TILE, (j + 1) * MXU_TILE)
        xbuf[SUBLANES:SUBLANES + TM_IN, wide] = proj(3 * ATTN_WIDTH + j * MXU_TILE, MXU_TILE)
        u = cb[:, wide]
        for kk in range(CONV_WIDTH):
            off = SUBLANES - (CONV_WIDTH - 1) + kk
            u = u + xbuf[off:off + TM_IN, wide] * cw_ref[kk:kk + 1, wide]
        xbuf[0:SUBLANES, wide] = xbuf[TM_IN:TM_IN + SUBLANES, wide]
        return u

    def gates(u, j):
        wide = slice(j * MXU_TILE, (j + 1) * MXU_TILE)
        uj = u.astype(BF16)
        r = jax.nn.sigmoid(_dot(uj, wrg_ref[j]) + brg[:, wide])
        ig = jax.nn.sigmoid(_dot(uj, wig_ref[j]) + big[:, wide])
        log_a = (-LRU_C * r) * sp_lam[:, wide]
        a_w = jnp.exp(log_a)
        mult = jnp.sqrt(-jnp.tanh(log_a) * (1.0 + a_w * a_w))
        return a_w, mult * (ig * u)

    def scan(a_w, b_w, gl, j):
        tiles = range(MXU_TILE // LANES)
        ab = []
        for jj in tiles:
            a = a_w[:, jj * LANES:(jj + 1) * LANES].reshape(n_vregs, SUBLANES, LANES)
            b = b_w[:, jj * LANES:(jj + 1) * LANES].reshape(n_vregs, SUBLANES, LANES)
            d = 1
            while d < SUBLANES:
                below = sub >= d
                b = jnp.where(below, b + a * pltpu.roll(b, d, axis=1), b)
                a = jnp.where(below, a * pltpu.roll(a, d, axis=1), a)
                d *= 2
            ab.append((a, b))
        cols = [slice(j * MXU_TILE + jj * LANES, j * MXU_TILE + (jj + 1) * LANES) for jj in tiles]
        st = [jnp.broadcast_to(state_s[0:1, cols[jj]], (SUBLANES, LANES)) for jj in tiles]
        hs = [[] for _ in tiles]
        for m in range(n_vregs):
            for jj in tiles:
                h = ab[jj][0][m] * st[jj] + ab[jj][1][m]
                hs[jj].append(h)
                st[jj] = jnp.broadcast_to(h[SUBLANES - 1:SUBLANES, :], (SUBLANES, LANES))
        for jj in tiles:
            state_s[0:1, cols[jj]] = st[jj][0:1, :]
            h_all = jnp.concatenate(hs[jj], axis=0)
            g = gl[:, jj * LANES:(jj + 1) * LANES]
            lru_ref[0, :, cols[jj]] = (h_all * jax.nn.gelu(g)).astype(BF16)

    u0 = conv(0)
    u1 = conv(1)
    attn_chunk(0, 0)
    ab0 = gates(u0, 0)
    gl0 = proj(4 * ATTN_WIDTH, MXU_TILE)
    attn_chunk(0, 1)
    scan(*ab0, gl0, 0)
    attn_chunk(1, 0)
    ab1 = gates(u1, 1)
    gl1 = proj(4 * ATTN_WIDTH + MXU_TILE, MXU_TILE)
    attn_chunk(2, 0)
    attn_chunk(2, 1)
    scan(*ab1, gl1, 1)
    attn_chunk(1, 1)


def _layer_spec(l, *shape):
    return pl.BlockSpec((None,) + shape, lambda *_: (l,) + (0,) * len(shape))


def _whole_spec(a):
    return pl.BlockSpec(a.shape, lambda *_: (0,) * a.ndim)


def _mixer_in(l, x, g1, w_in, qg, kg, pavg, conv_w, conv_b, wrg, brg, wig, big, lam):
    B, S, _ = x.shape
    tok = lambda b, t: (b, t, 0)
    out_sd = jax.ShapeDtypeStruct((B, S, ATTN_WIDTH), BF16)
    n_wide = LRU_WIDTH // MXU_TILE
    return pl.pallas_call(
        functools.partial(_mixer_in_kernel, layer=l),
        grid=(B, S // TM_IN),
        in_specs=[
            pl.BlockSpec((1, TM_IN, D_MODEL), tok),
            _whole_spec(g1),
            _layer_spec(l, D_MODEL, IN_WIDTH),
            _whole_spec(qg),
            _whole_spec(kg),
            _whole_spec(pavg),
            _layer_spec(l, CONV_WIDTH, LRU_WIDTH),
            _whole_spec(conv_b),
            _layer_spec(l, n_wide, MXU_TILE, MXU_TILE),
            _whole_spec(brg),
            _layer_spec(l, n_wide, MXU_TILE, MXU_TILE),
            _whole_spec(big),
            _whole_spec(lam),
        ],
        out_specs=[pl.BlockSpec((1, TM_IN, ATTN_WIDTH), tok)] * 4,
        out_shape=[out_sd] * 4,
        scratch_shapes=[
            pltpu.VMEM((TM_IN + SUBLANES, LRU_WIDTH), F32),
            pltpu.VMEM((SUBLANES, LRU_WIDTH), F32),
        ],
        compiler_params=pltpu.CompilerParams(
            dimension_semantics=("arbitrary", "arbitrary"),
            vmem_limit_bytes=VMEM_LIMIT),
        name="mixer_in",
    )(x, g1, w_in, qg, kg, pavg, conv_w, conv_b, wrg, brg, wig, big, lam)


def _sb_attn_kernel(q_ref, k_ref, v_ref, u_ref, o_ref, qs_ref, carry_ref, acc_ref, go_ref):
    i = pl.program_id(2)
    lane = lax.broadcasted_iota(jnp.int32, (1, LANES), 1)
    pair_rows = 2 * TQ
    in_a = jnp.where(lane < HEAD_DIM, 1.0, 0.0).astype(BF16)
    in_b = jnp.where(lane >= HEAD_DIM, 1.0, 0.0).astype(BF16)
    for p in range(PAIRS):
        q2 = q_ref[0, :, p * LANES:(p + 1) * LANES]
        r0 = p * pair_rows
        qs_ref[r0:r0 + TQ, :] = q2 * in_a
        qs_ref[r0 + TQ:r0 + pair_rows, :] = q2 * in_b
    n_chunks = (PAIRS * pair_rows) // ROW_CHUNK
    col_minus_row = (lax.broadcasted_iota(jnp.int32, (ROW_CHUNK, HALF), 1)
                     - lax.broadcasted_iota(jnp.int32, (ROW_CHUNK, HALF), 0))

    def step(start, nb, diag):
        nh = 2 * nb
        kblk = [k_ref[0, pl.ds(start, nb * TK), p * LANES:(p + 1) * LANES]
                for p in range(PAIRS)]
        vblk = [v_ref[0, pl.ds(start, nb * TK), p * LANES:(p + 1) * LANES]
                for p in range(PAIRS)]
        grp_rows = GROUP_CHUNKS * ROW_CHUNK
        zg = [_dot_nt(qs_ref[g * grp_rows:(g + 1) * grp_rows, :],
                      kblk[g * grp_rows // pair_rows])
              for g in range(PAIRS * pair_rows // grp_rows)]

        def zpiece(c, h):
            r0 = (c % GROUP_CHUNKS) * ROW_CHUNK
            return zg[c // GROUP_CHUNKS][r0:r0 + ROW_CHUNK, h * HALF:(h + 1) * HALF]

        def piece_mask(c, h):
            if not diag or h < nh - 2:
                return None
            q0 = (c * ROW_CHUNK) % TQ
            k0 = (h - (nh - 2)) * HALF
            if k0 + HALF - 1 < q0:
                return None
            if k0 >= q0 + ROW_CHUNK - 1:
                return False
            return col_minus_row < (q0 - k0)

        def suffix_sums(chunks):
            pieces, offset = [], {}
            for b in reversed(range(nb)):
                for c in chunks:
                    sps = []
                    for h in (2 * b, 2 * b + 1):
                        keep = piece_mask(c, h)
                        if keep is False:
                            sps.append(jnp.zeros((ROW_CHUNK, HALF), BF16))
                            continue
                        zp = zpiece(c, h)
                        pos = jnp.maximum(zp, 0.0)
                        neg_abs = (zp - pos) - pos
                        sp = pos + jnp.log2(1.0 + jnp.exp2(neg_abs))
                        if keep is not None:
                            sp = jnp.where(keep, sp, 0.0)
                        sps.append(sp.astype(BF16))
                    offset[(b, c)] = len(pieces) * ROW_CHUNK
                    pieces.append(jnp.concatenate(sps, axis=1))
            xx = jnp.concatenate(pieces, axis=0)
            return _dot(xx, u_ref[...]), offset

        def weights(chunks, rr, offset):
            ws, carries = [], []
            for c in chunks:
                rows = slice(c * ROW_CHUNK, (c + 1) * ROW_CHUNK)
                carry = None if diag else carry_ref[rows, :]
                wh = [None] * nh
                for b in reversed(range(nb)):
                    base = offset[(b, c)]
                    suf = rr[base:base + ROW_CHUNK, :]
                    tot = jnp.broadcast_to(suf[:, 0:1], (ROW_CHUNK, HALF))
                    for h in (2 * b + 1, 2 * b):
                        keep = piece_mask(c, h)
                        if keep is False:
                            wh[h] = jnp.zeros((ROW_CHUNK, HALF), F32)
                            continue
                        lw = zpiece(c, h) + suf[:, (h % 2) * HALF:(h % 2 + 1) * HALF]
                        if carry is not None:
                            lw = lw + carry
                        w = jnp.exp2(lw)
                        if keep is not None:
                            w = jnp.where(keep, w, 0.0)
                        wh[h] = w
                    carry = tot if carry is None else carry + tot
                carry_ref[rows, :] = carry
                carries.append(carry)
                ws.append(jnp.concatenate(wh, axis=1).astype(BF16))
            return jnp.concatenate(ws, axis=0), carries

        def finish(chunks, ww):
            rows = slice(chunks[0] * ROW_CHUNK, (chunks[-1] + 1) * ROW_CHUNK)
            pv = _dot(ww, vblk[chunks[0] * ROW_CHUNK // pair_rows])
            if diag:
                acc_ref[rows, :] = pv
            else:
                acc_ref[rows, :] += pv

        groups = [list(range(g, g + GROUP_CHUNKS)) for g in range(0, n_chunks, GROUP_CHUNKS)]
        all_carries, sums = [], {}
        for k in range(len(groups) + 1):
            if k < len(groups):
                sums[k] = suffix_sums(groups[k])
            if k >= 1:
                ww, carries = weights(groups[k - 1], *sums.pop(k - 1))
                all_carries += carries
                if k == len(groups):
                    cmax = functools.reduce(jnp.maximum, all_carries)
                    go_ref[0] = (jnp.max(cmax) > LOG2_W_UNDERFLOW).astype(jnp.int32)
                finish(groups[k - 1], ww)

    @pl.when(i == 0)
    def _():
        step(0, 1, True)

    @pl.when(i > 0)
    def _():
        step(pl.multiple_of((i - 1) * TK, TK), 2, True)

    n_rest = jnp.maximum(i - 1, 0)

    def cond(state):
        jj, go = state
        return jnp.logical_and(jj < n_rest, go > 0)

    def body(state):
        jj, _ = state
        step(pl.multiple_of((n_rest - 1 - jj) * TK, TK), 1, False)
        return jj + 1, go_ref[0]

    lax.while_loop(cond, body, (jnp.int32(0), go_ref[0]))

    for p in range(PAIRS):
        r0 = p * pair_rows
        o_ref[0, :, p * LANES:(p + 1) * LANES] = jnp.where(
            lane < HEAD_DIM, acc_ref[r0:r0 + TQ, :], acc_ref[r0 + TQ:r0 + pair_rows, :]).astype(BF16)


def _sb_attn(q, k, v, umat):
    B, S, _ = q.shape
    width = PAIRS * LANES
    stacked = PAIRS * 2 * TQ
    return pl.pallas_call(
        _sb_attn_kernel,
        grid=(B, ATTN_WIDTH // width, S // TQ),
        in_specs=[
            pl.BlockSpec((1, TQ, width), lambda b, p, i: (b, i, p)),
            pl.BlockSpec((1, S, width), lambda b, p, i: (b, 0, p)),
            pl.BlockSpec((1, S, width), lambda b, p, i: (b, 0, p)),
            pl.BlockSpec((TK, TK), lambda b, p, i: (0, 0)),
        ],
        out_specs=pl.BlockSpec((1, TQ, width), lambda b, p, i: (b, i, p)),
        out_shape=jax.ShapeDtypeStruct((B, S, ATTN_WIDTH), BF16),
        scratch_shapes=[
            pltpu.VMEM((stacked, LANES), BF16),
            pltpu.VMEM((stacked, LANES), F32),
            pltpu.VMEM((stacked, LANES), F32),
            pltpu.SMEM((1,), jnp.int32),
        ],
        compiler_params=pltpu.CompilerParams(
            dimension_semantics=("arbitrary", "arbitrary", "arbitrary"),
            vmem_limit_bytes=VMEM_LIMIT),
        name="sb_attn",
    )(q, k, v, umat)


def _out_mlp_kernel(x_ref, attn_ref, lru_ref, wo_ref, g2_ref, wup_ref, wdn_ref, o_ref, h_s, *,
                    layer):
    @pl.when(pl.program_id(1) == 0)
    def _():
        for r in range(2):
            rows = slice(r * (TM_MLP // 2), (r + 1) * (TM_MLP // 2))
            x1 = (x_ref[rows, :] + _dot(attn_ref[rows, :], wo_ref[0:ATTN_WIDTH, :])
                  + _dot(lru_ref[rows, :], wo_ref[ATTN_WIDTH:, :]))
            ms = jnp.mean(x1 * x1, axis=-1, keepdims=True)
            h_s[rows, :] = ((x1 * lax.rsqrt(ms + EPS)) * g2_ref[layer:layer + 1, :]).astype(BF16)
            o_ref[rows, :] = x1

    for c in range(FF_CHUNK // FF_INNER):
        cols = slice(c * FF_INNER, (c + 1) * FF_INNER)
        up = jnp.maximum(_dot(h_s[...], wup_ref[:, cols]), 0.0)
        o_ref[...] += _dot((up * up).astype(BF16), wdn_ref[cols, :])


def _out_mlp(l, x, attn, lru, w_out, g2, w_up, w_down):
    n_tok = x.shape[0]
    tok = lambda t, f: (t, 0)
    return pl.pallas_call(
        functools.partial(_out_mlp_kernel, layer=l),
        grid=(n_tok // TM_MLP, D_FF // FF_CHUNK),
        in_specs=[
            pl.BlockSpec((TM_MLP, D_MODEL), tok),
            pl.BlockSpec((TM_MLP, ATTN_WIDTH), tok),
            pl.BlockSpec((TM_MLP, LRU_WIDTH), tok),
            _layer_spec(l, D_MODEL, D_MODEL),
            _whole_spec(g2),
            pl.BlockSpec((None, D_MODEL, FF_CHUNK), lambda t, f: (l, 0, f)),
            pl.BlockSpec((None, FF_CHUNK, D_MODEL), lambda t, f: (l, f, 0)),
        ],
        out_specs=pl.BlockSpec((TM_MLP, D_MODEL), tok),
        out_shape=jax.ShapeDtypeStruct((n_tok, D_MODEL), F32),
        scratch_shapes=[pltpu.VMEM((TM_MLP, D_MODEL), BF16)],
        compiler_params=pltpu.CompilerParams(
            dimension_semantics=("arbitrary", "arbitrary"),
            vmem_limit_bytes=VMEM_LIMIT),
        name="out_mlp",
    )(x, attn, lru, w_out, g2, w_up, w_down)


def _packed_block_diag(w):
    per = MXU_TILE // LRU_BLOCK
    depth, n_blocks = w.shape[:2]
    w = w.reshape(depth, n_blocks // per, per, LRU_BLOCK, LRU_BLOCK)
    eye = jnp.eye(per, dtype=w.dtype)
    out = w[:, :, :, :, None, :] * eye[None, None, :, None, :, None]
    return out.reshape(depth, n_blocks // per, MXU_TILE, MXU_TILE)


def _suffix_matrix():
    j = jnp.arange(TK)[:, None]
    s = jnp.arange(TK)[None, :]
    return jnp.where(j >= s, -1.0, 0.0).astype(BF16)


def kernel(x, norm1_g, w_in, conv_w, conv_b, w_rg, b_rg, w_ig, b_ig, lru_lambda, q_norm_g,
           k_norm_g, w_out, norm2_g, w_up, w_down):
    B, S, D = x.shape
    depth = w_in.shape[0]
    head = jnp.arange(MXU_TILE) // HEAD_DIM
    pavg = jnp.where(head[:, None] == head[None, :], 1.0 / HEAD_DIM, 0.0).astype(BF16)
    umat = _suffix_matrix()
    scale = LOG2_E / math.sqrt(HEAD_DIM)
    per_tile = (1, MXU_TILE // HEAD_DIM)
    mixer_params = (
        norm1_g, w_in.astype(BF16), jnp.tile(q_norm_g, per_tile) * scale,
        jnp.tile(k_norm_g, per_tile), pavg, conv_w, conv_b,
        _packed_block_diag(w_rg).astype(BF16), b_rg,
        _packed_block_diag(w_ig).astype(BF16), b_ig, lru_lambda)
    mlp_params = (w_out.astype(BF16), norm2_g, w_up.astype(BF16), w_down.astype(BF16))
    for l in range(depth):
        q, k, v, lru = _mixer_in(l, x, *mixer_params)
        attn = _sb_attn(q, k, v, umat)
        x = _out_mlp(
            l, x.reshape(B * S, D), attn.reshape(B * S, ATTN_WIDTH),
            lru.reshape(B * S, LRU_WIDTH), *mlp_params).reshape(B, S, D)
    return x
```

```python
import functools
import math

import jax
import jax.numpy as jnp
from jax import lax
from jax.experimental import pallas as pl
from jax.experimental.pallas import tpu as pltpu

D_MODEL = 1024
HEAD_DIM = 64
ATTN_WIDTH = 512
LRU_WIDTH = 512
N_LRU_BLOCKS = 8
LRU_BLOCK = 64
IN_WIDTH = 3 * ATTN_WIDTH + 2 * LRU_WIDTH
CONV_WIDTH = 4
LRU_C = 8.0
D_FF = 4 * D_MODEL
EPS = 1e-6

LANES = 128
SUBLANES = 8
BF16_SUBLANES = 16
VMEM_LIMIT = 56 * 1024 * 1024

MXU_TILE = 256
TM_IN = 512
TM_MLP = 1024
FF_CHUNK = 2048
FF_INNER = 1024
PAIRS = 4
TQ = 256
TK = 256
HALF = TK // 2
ROW_CHUNK = 64
GROUP_CHUNKS = 8
LOG2_W_UNDERFLOW = -130.0
LOG2_E = 1.4426950408889634

F32 = jnp.float32
BF16 = jnp.bfloat16


def _dot(a, b):
    return jnp.dot(a, b, preferred_element_type=F32)


def _dot_nt(a, b):
    return lax.dot_general(a, b, (((1,), (1,)), ((), ())), preferred_element_type=F32)


def _softplus(x):
    return jnp.maximum(x, 0.0) + jnp.log(1.0 + jnp.exp(-jnp.abs(x)))


def _mixer_in_kernel(x_ref, g1_ref, win_ref, qg_ref, kg_ref, pavg_ref, cw_ref, cb_ref,
                     wrg_ref, brg_ref, wig_ref, big_ref, lam_ref, wup_ref, wdn_ref,
                     q_ref, k_ref, v_ref, lru_ref, wup_bf_ref, wdn_bf_ref,
                     xbuf, state_s, *, layer):
    t = pl.program_id(1)
    wup_bf_ref[...] = wup_ref[...].astype(BF16)
    wdn_bf_ref[...] = wdn_ref[...].astype(BF16)
    row = slice(layer, layer + 1)
    g1, qg, kg, cb = g1_ref[row, :], qg_ref[row, :], kg_ref[row, :], cb_ref[row, :]
    brg, big, lam = brg_ref[row, :], big_ref[row, :], lam_ref[row, :]

    @pl.when(t == 0)
    def _():
        xbuf[0:SUBLANES, :] = jnp.zeros((SUBLANES, LRU_WIDTH), F32)
        state_s[...] = jnp.zeros_like(state_s)

    x = x_ref[0]
    ms = jnp.mean(x * x, axis=-1, keepdims=True)
    y = ((x * lax.rsqrt(ms + EPS)) * g1).astype(BF16)

    def proj(col0, width):
        return _dot(y, win_ref[:, col0:col0 + width])

    n_vregs = TM_IN // SUBLANES
    sub = lax.broadcasted_iota(jnp.int32, (n_vregs, SUBLANES, LANES), 1)
    sp_lam = _softplus(-lam)
    pavg = pavg_ref[...]

    def attn_chunk(c, j):
        cols = slice(j * MXU_TILE, (j + 1) * MXU_TILE)
        ps = proj(c * ATTN_WIDTH + j * MXU_TILE, MXU_TILE)
        if c == 2:
            v_ref[0, :, cols] = ps.astype(BF16)
            return
        gain, o_ref = ((qg, q_ref), (kg, k_ref))[c]
        m = _dot((ps * ps).astype(BF16), pavg)
        o_ref[0, :, cols] = ((ps * lax.rsqrt(m + EPS)) * gain).astype(BF16)

    def conv(j):
        wide = slice(j * MXU_TILE, (j + 1) * MXU_TILE)
        xbuf[SUBLANES:SUBLANES + TM_IN, wide] = proj(3 * ATTN_WIDTH + j * MXU_TILE, MXU_TILE)
        u = cb[:, wide]
        for kk in range(CONV_WIDTH):
            off = SUBLANES - (CONV_WIDTH - 1) + kk
            u = u + xbuf[off:off + TM_IN, wide] * cw_ref[kk:kk + 1, wide]
        xbuf[0:SUBLANES, wide] = xbuf[TM_IN:TM_IN + SUBLANES, wide]
        return u

    def gates(u, j):
        wide = slice(j * MXU_TILE, (j + 1) * MXU_TILE)
        uj = u.astype(BF16)
        r = jax.nn.sigmoid(_dot(uj, wrg_ref[j]) + brg[:, wide])
        ig = jax.nn.sigmoid(_dot(uj, wig_ref[j]) + big[:, wide])
        log_a = (-LRU_C * r) * sp_lam[:, wide]
        a_w = jnp.exp(log_a)
        mult = jnp.sqrt(-jnp.tanh(log_a) * (1.0 + a_w * a_w))
        return a_w, mult * (ig * u)

    def scan(a_w, b_w, gl, j):
        tiles = range(MXU_TILE // LANES)
        ab = []
        for jj in tiles:
            a = a_w[:, jj * LANES:(jj + 1) * LANES].reshape(n_vregs, SUBLANES, LANES)
            b = b_w[:, jj * LANES:(jj + 1) * LANES].reshape(n_vregs, SUBLANES, LANES)
            d = 1
            while d < SUBLANES:
                below = sub >= d
                b = jnp.where(below, b + a * pltpu.roll(b, d, axis=1), b)
                a = jnp.where(below, a * pltpu.roll(a, d, axis=1), a)
                d *= 2
            ab.append((a, b))
        cols = [slice(j * MXU_TILE + jj * LANES, j * MXU_TILE + (jj + 1) * LANES) for jj in tiles]
        st = [jnp.broadcast_to(state_s[0:1, cols[jj]], (SUBLANES, LANES)) for jj in tiles]
        hs = [[] for _ in tiles]
        for m in range(n_vregs):
            for jj in tiles:
                h = ab[jj][0][m] * st[jj] + ab[jj][1][m]
                hs[jj].append(h)
                st[jj] = jnp.broadcast_to(h[SUBLANES - 1:SUBLANES, :], (SUBLANES, LANES))
        for jj in tiles:
            state_s[0:1, cols[jj]] = st[jj][0:1, :]
            h_all = jnp.concatenate(hs[jj], axis=0)
            g = gl[:, jj * LANES:(jj + 1) * LANES]
            lru_ref[0, :, cols[jj]] = (h_all * jax.nn.gelu(g)).astype(BF16)

    u0 = conv(0)
    u1 = conv(1)
    attn_chunk(0, 0)
    ab0 = gates(u0, 0)
    gl0 = proj(4 * ATTN_WIDTH, MXU_TILE)
    attn_chunk(0, 1)
    scan(*ab0, gl0, 0)
    attn_chunk(1, 0)
    ab1 = gates(u1, 1)
    gl1 = proj(4 * ATTN_WIDTH + MXU_TILE, MXU_TILE)
    attn_chunk(2, 0)
    attn_chunk(2, 1)
    scan(*ab1, gl1, 1)
    attn_chunk(1, 1)


def _layer_spec(l, *shape):
    return pl.BlockSpec((None,) + shape, lambda *_: (l,) + (0,) * len(shape))


def _whole_spec(a):
    return pl.BlockSpec(a.shape, lambda *_: (0,) * a.ndim)


def _mixer_in(l, x, g1, w_in, qg, kg, pavg, conv_w, conv_b, wrg, brg, wig, big, lam, w_up, w_down):
    B, S, _ = x.shape
    tiles = S // TM_IN
    tok = lambda b, t: (b, t, 0)
    slab = lambda b, t: (b * tiles + t, 0)
    out_sd = jax.ShapeDtypeStruct((B, S, ATTN_WIDTH), BF16)
    n_wide = LRU_WIDTH // MXU_TILE
    up_rows = D_MODEL // (B * tiles)
    dn_rows = D_FF // (B * tiles)
    assert up_rows % BF16_SUBLANES == 0 and dn_rows % BF16_SUBLANES == 0
    return pl.pallas_call(
        functools.partial(_mixer_in_kernel, layer=l),
        grid=(B, S // TM_IN),
        in_specs=[
            pl.BlockSpec((1, TM_IN, D_MODEL), tok),
            _whole_spec(g1),
            _layer_spec(l, D_MODEL, IN_WIDTH),
            _whole_spec(qg),
            _whole_spec(kg),
            _whole_spec(pavg),
            _layer_spec(l, CONV_WIDTH, LRU_WIDTH),
            _whole_spec(conv_b),
            _layer_spec(l, n_wide, MXU_TILE, MXU_TILE),
            _whole_spec(brg),
            _layer_spec(l, n_wide, MXU_TILE, MXU_TILE),
            _whole_spec(big),
            _whole_spec(lam),
            pl.BlockSpec((None, up_rows, D_FF), lambda b, t: (l, b * tiles + t, 0)),
            pl.BlockSpec((None, dn_rows, D_MODEL), lambda b, t: (l, b * tiles + t, 0)),
        ],
        out_specs=[pl.BlockSpec((1, TM_IN, ATTN_WIDTH), tok)] * 4
        + [pl.BlockSpec((up_rows, D_FF), slab), pl.BlockSpec((dn_rows, D_MODEL), slab)],
        out_shape=[out_sd] * 4 + [jax.ShapeDtypeStruct((D_MODEL, D_FF), BF16),
                                  jax.ShapeDtypeStruct((D_FF, D_MODEL), BF16)],
        scratch_shapes=[
            pltpu.VMEM((TM_IN + SUBLANES, LRU_WIDTH), F32),
            pltpu.VMEM((SUBLANES, LRU_WIDTH), F32),
        ],
        compiler_params=pltpu.CompilerParams(
            dimension_semantics=("arbitrary", "arbitrary"),
            vmem_limit_bytes=VMEM_LIMIT),
        name="mixer_in",
    )(x, g1, w_in, qg, kg, pavg, conv_w, conv_b, wrg, brg, wig, big, lam, w_up, w_down)


def _sb_attn_kernel(q_ref, k_ref, v_ref, u_ref, o_ref, qs_ref, carry_ref, acc_ref, go_ref):
    i = pl.program_id(2)
    lane = lax.broadcasted_iota(jnp.int32, (1, LANES), 1)
    pair_rows = 2 * TQ
    in_a = jnp.where(lane < HEAD_DIM, 1.0, 0.0).astype(BF16)
    in_b = jnp.where(lane >= HEAD_DIM, 1.0, 0.0).astype(BF16)
    for p in range(PAIRS):
        q2 = q_ref[0, :, p * LANES:(p + 1) * LANES]
        r0 = p * pair_rows
        qs_ref[r0:r0 + TQ, :] = q2 * in_a
        qs_ref[r0 + TQ:r0 + pair_rows, :] = q2 * in_b
    n_chunks = (PAIRS * pair_rows) // ROW_CHUNK
    col_minus_row = (lax.broadcasted_iota(jnp.int32, (ROW_CHUNK, HALF), 1)
                     - lax.broadcasted_iota(jnp.int32, (ROW_CHUNK, HALF), 0))

    def step(start, nb, diag):
        nh = 2 * nb
        kblk = [k_ref[0, pl.ds(start, nb * TK), p * LANES:(p + 1) * LANES]
                for p in range(PAIRS)]
        vblk = [v_ref[0, pl.ds(start, nb * TK), p * LANES:(p + 1) * LANES]
                for p in range(PAIRS)]
        grp_rows = GROUP_CHUNKS * ROW_CHUNK
        zg = [_dot_nt(qs_ref[g * grp_rows:(g + 1) * grp_rows, :],
                      kblk[g * grp_rows // pair_rows])
              for g in range(PAIRS * pair_rows // grp_rows)]

        def zpiece(c, h):
            r0 = (c % GROUP_CHUNKS) * ROW_CHUNK
            return zg[c // GROUP_CHUNKS][r0:r0 + ROW_CHUNK, h * HALF:(h + 1) * HALF]

        def piece_mask(c, h):
            if not diag or h < nh - 2:
                return None
            q0 = (c * ROW_CHUNK) % TQ
            k0 = (h - (nh - 2)) * HALF
            if k0 + HALF - 1 < q0:
                return None
            if k0 >= q0 + ROW_CHUNK - 1:
                return False
            return col_minus_row < (q0 - k0)

        def suffix_sums(chunks):
            pieces, offset = [], {}
            for b in reversed(range(nb)):
                for c in chunks:
                    sps = []
                    for h in (2 * b, 2 * b + 1):
                        keep = piece_mask(c, h)
                        if keep is False:
                            sps.append(jnp.zeros((ROW_CHUNK, HALF), BF16))
                            continue
                        zp = zpiece(c, h)
                        pos = jnp.maximum(zp, 0.0)
                        neg_abs = (zp - pos) - pos
                        sp = pos + jnp.log2(1.0 + jnp.exp2(neg_abs))
                        if keep is not None:
                            sp = jnp.where(keep, sp, 0.0)
                        sps.append(sp.astype(BF16))
                    offset[(b, c)] = len(pieces) * ROW_CHUNK
                    pieces.append(jnp.concatenate(sps, axis=1))
            xx = jnp.concatenate(pieces, axis=0)
            return _dot(xx, u_ref[...]), offset

        def weights(chunks, rr, offset):
            ws, carries = [], []
            for c in chunks:
                rows = slice(c * ROW_CHUNK, (c + 1) * ROW_CHUNK)
                carry = None if diag else carry_ref[rows, :]
                wh = [None] * nh
                for b in reversed(range(nb)):
                    base = offset[(b, c)]
                    suf = rr[base:base + ROW_CHUNK, :]
                    tot = jnp.broadcast_to(suf[:, 0:1], (ROW_CHUNK, HALF))
                    for h in (2 * b + 1, 2 * b):
                        keep = piece_mask(c, h)
                        if keep is False:
                            wh[h] = jnp.zeros((ROW_CHUNK, HALF), F32)
                            continue
                        lw = zpiece(c, h) + suf[:, (h % 2) * HALF:(h % 2 + 1) * HALF]
                        if carry is not None:
                            lw = lw + carry
                        w = jnp.exp2(lw)
                        if keep is not None:
                            w = jnp.where(keep, w, 0.0)
                        wh[h] = w
                    carry = tot if carry is None else carry + tot
                carry_ref[rows, :] = carry
                carries.append(carry)
                ws.append(jnp.concatenate(wh, axis=1).astype(BF16))
            return jnp.concatenate(ws, axis=0), carries

        def finish(chunks, ww):
            rows = slice(chunks[0] * ROW_CHUNK, (chunks[-1] + 1) * ROW_CHUNK)
            pv = _dot(ww, vblk[chunks[0] * ROW_CHUNK // pair_rows])
            if diag:
                acc_ref[rows, :] = pv
            else:
                acc_ref[rows, :] += pv

        groups = [list(range(g, g + GROUP_CHUNKS)) for g in range(0, n_chunks, GROUP_CHUNKS)]
        all_carries, sums = [], {}
        for k in range(len(groups) + 1):
            if k < len(groups):
                sums[k] = suffix_sums(groups[k])
            if k >= 1:
                ww, carries = weights(groups[k - 1], *sums.pop(k - 1))
                all_carries += carries
                if k == len(groups):
                    cmax = functools.reduce(jnp.maximum, all_carries)
                    go_ref[0] = (jnp.max(cmax) > LOG2_W_UNDERFLOW).astype(jnp.int32)
                finish(groups[k - 1], ww)

    @pl.when(i == 0)
    def _():
        step(0, 1, True)

    @pl.when(i > 0)
    def _():
        step(pl.multiple_of((i - 1) * TK, TK), 2, True)

    n_rest = jnp.maximum(i - 1, 0)

    def cond(state):
        jj, go = state
        return jnp.logical_and(jj < n_rest, go > 0)

    def body(state):
        jj, _ = state
        step(pl.multiple_of((n_rest - 1 - jj) * TK, TK), 1, False)
        return jj + 1, go_ref[0]

    lax.while_loop(cond, body, (jnp.int32(0), go_ref[0]))

    for p in range(PAIRS):
        r0 = p * pair_rows
        o_ref[0, :, p * LANES:(p + 1) * LANES] = jnp.where(
            lane < HEAD_DIM, acc_ref[r0:r0 + TQ, :], acc_ref[r0 + TQ:r0 + pair_rows, :]).astype(BF16)


def _sb_attn(q, k, v, umat):
    B, S, _ = q.shape
    width = PAIRS * LANES
    stacked = PAIRS * 2 * TQ
    return pl.pallas_call(
        _sb_attn_kernel,
        grid=(B, ATTN_WIDTH // width, S // TQ),
        in_specs=[
            pl.BlockSpec((1, TQ, width), lambda b, p, i: (b, i, p)),
            pl.BlockSpec((1, S, width), lambda b, p, i: (b, 0, p)),
            pl.BlockSpec((1, S, width), lambda b, p, i: (b, 0, p)),
            pl.BlockSpec((TK, TK), lambda b, p, i: (0, 0)),
        ],
        out_specs=pl.BlockSpec((1, TQ, width), lambda b, p, i: (b, i, p)),
        out_shape=jax.ShapeDtypeStruct((B, S, ATTN_WIDTH), BF16),
        scratch_shapes=[
            pltpu.VMEM((stacked, LANES), BF16),
            pltpu.VMEM((stacked, LANES), F32),
            pltpu.VMEM((stacked, LANES), F32),
            pltpu.SMEM((1,), jnp.int32),
        ],
        compiler_params=pltpu.CompilerParams(
            dimension_semantics=("arbitrary", "arbitrary", "arbitrary"),
            vmem_limit_bytes=VMEM_LIMIT),
        name="sb_attn",
    )(q, k, v, umat)


def _out_mlp_kernel(x_ref, attn_ref, lru_ref, wo_ref, g2_ref, wup_ref, wdn_ref, o_ref, h_s, *,
                    layer):
    @pl.when(pl.program_id(1) == 0)
    def _():
        for r in range(2):
            rows = slice(r * (TM_MLP // 2), (r + 1) * (TM_MLP // 2))
            x1 = (x_ref[rows, :] + _dot(attn_ref[rows, :], wo_ref[0:ATTN_WIDTH, :])
                  + _dot(lru_ref[rows, :], wo_ref[ATTN_WIDTH:, :]))
            ms = jnp.mean(x1 * x1, axis=-1, keepdims=True)
            h_s[rows, :] = ((x1 * lax.rsqrt(ms + EPS)) * g2_ref[layer:layer + 1, :]).astype(BF16)
            o_ref[rows, :] = x1

    for c in range(FF_CHUNK // FF_INNER):
        cols = slice(c * FF_INNER, (c + 1) * FF_INNER)
        up = jnp.maximum(_dot(h_s[...], wup_ref[:, cols]), 0.0)
        o_ref[...] += _dot((up * up).astype(BF16), wdn_ref[cols, :])


def _out_mlp(l, x, attn, lru, w_out, g2, w_up, w_down):
    n_tok = x.shape[0]
    tok = lambda t, f: (t, 0)
    return pl.pallas_call(
        functools.partial(_out_mlp_kernel, layer=l),
        grid=(n_tok // TM_MLP, D_FF // FF_CHUNK),
        in_specs=[
            pl.BlockSpec((TM_MLP, D_MODEL), tok),
            pl.BlockSpec((TM_MLP, ATTN_WIDTH), tok),
            pl.BlockSpec((TM_MLP, LRU_WIDTH), tok),
            _layer_spec(l, D_MODEL, D_MODEL),
            _whole_spec(g2),
            pl.BlockSpec((D_MODEL, FF_CHUNK), lambda t, f: (0, f)),
            pl.BlockSpec((FF_CHUNK, D_MODEL), lambda t, f: (f, 0)),
        ],
        out_specs=pl.BlockSpec((TM_MLP, D_MODEL), tok),
        out_shape=jax.ShapeDtypeStruct((n_tok, D_MODEL), F32),
        scratch_shapes=[pltpu.VMEM((TM_MLP, D_MODEL), BF16)],
        compiler_params=pltpu.CompilerParams(
            dimension_semantics=("arbitrary", "arbitrary"),
            vmem_limit_bytes=VMEM_LIMIT),
        name="out_mlp",
    )(x, attn, lru, w_out, g2, w_up, w_down)


def _packed_block_diag(w):
    per = MXU_TILE // LRU_BLOCK
    depth, n_blocks = w.shape[:2]
    w = w.reshape(depth, n_blocks // per, per, LRU_BLOCK, LRU_BLOCK)
    eye = jnp.eye(per, dtype=w.dtype)
    out = w[:, :, :, :, None, :] * eye[None, None, :, None, :, None]
    return out.reshape(depth, n_blocks // per, MXU_TILE, MXU_TILE)


def _suffix_matrix():
    j = jnp.arange(TK)[:, None]
    s = jnp.arange(TK)[None, :]
    return jnp.where(j >= s, -1.0, 0.0).astype(BF16)


def kernel(x, norm1_g, w_in, conv_w, conv_b, w_rg, b_rg, w_ig, b_ig, lru_lambda, q_norm_g,
           k_norm_g, w_out, norm2_g, w_up, w_down):
    B, S, D = x.shape
    depth = w_in.shape[0]
    head = jnp.arange(MXU_TILE) // HEAD_DIM
    pavg = jnp.where(head[:, None] == head[None, :], 1.0 / HEAD_DIM, 0.0).astype(BF16)
    umat = _suffix_matrix()
    scale = LOG2_E / math.sqrt(HEAD_DIM)
    per_tile = (1, MXU_TILE // HEAD_DIM)
    mixer_params = (
        norm1_g, w_in.astype(BF16), jnp.tile(q_norm_g, per_tile) * scale,
        jnp.tile(k_norm_g, per_tile), pavg, conv_w, conv_b,
        _packed_block_diag(w_rg).astype(BF16), b_rg,
        _packed_block_diag(w_ig).astype(BF16), b_ig, lru_lambda, w_up, w_down)
    w_out_bf = w_out.astype(BF16)
    for l in range(depth):
        q, k, v, lru, w_up_bf, w_down_bf = _mixer_in(l, x, *mixer_params)
        attn = _sb_attn(q, k, v, umat)
        x = _out_mlp(
            l, x.reshape(B * S, D), attn.reshape(B * S, ATTN_WIDTH),
            lru.reshape(B * S, LRU_WIDTH), w_out_bf, norm2_g, w_up_bf, w_down_bf,
        ).reshape(B, S, D)
    return x
```
